```python
import math
import jax, jax.numpy as jnp
from jax import lax
import numpy as np

D_MODEL = 2048
BATCH = 8
SEQ = 2048
DEPTH = 1
DEC_BATCH = 32
DEC_SEQ = 4
PAST_LEN = 8192
PAGE_SIZE = 128

D_MIX = D_MODEL
ATTN_HEADS = 8
ATTN_HEAD_DIM = 128
D_ATTN = ATTN_HEADS * ATTN_HEAD_DIM
MOBA_BLOCK = 256
MOBA_TOPK = 3
ATTN_Q_BLOCK = 32
D_INNER = D_MIX - D_ATTN
SSM_HEAD_DIM = 64
SSM_HEADS = D_INNER // SSM_HEAD_DIM
SSM_GROUPS = 2
SSM_STATE = 128
CONV_WIDTH = 4
CONV_CH = D_INNER + 2 * SSM_GROUPS * SSM_STATE
SSD_CHUNK = 128
D_PROJ = 3 * D_ATTN + D_INNER + CONV_CH + SSM_HEADS
N_EXPERTS = 32
TOP_K = 4
D_FF = D_MODEL
SWIGLU_LIMIT = 7.0
SWIGLU_ALPHA = 1.702
MOE_BLOCK = 128
NORM_EPS = 1e-5
DEEPNORM_ALPHA = (2 * DEPTH) ** 0.25
DEEPNORM_BETA = (8 * DEPTH) ** -0.25

kernel_name = 'hymba_ssd_moba_moe_deepnorm_step'


def layer_norm(x, w, b):
    xf = x.astype(jnp.float32)
    mu = jnp.mean(xf, axis=-1, keepdims=True)
    var = jnp.mean(jnp.square(xf - mu), axis=-1, keepdims=True)
    return ((xf - mu) * lax.rsqrt(var + NORM_EPS) * w.astype(jnp.float32)
            + b.astype(jnp.float32)).astype(x.dtype)


def gated_group_rmsnorm(y, z, w):
    g = y.astype(jnp.float32) * jax.nn.silu(z.astype(jnp.float32))
    gs = g.reshape(g.shape[:-1] + (SSM_GROUPS, D_INNER // SSM_GROUPS))
    gs = gs * lax.rsqrt(jnp.mean(jnp.square(gs), axis=-1, keepdims=True) + NORM_EPS)
    return gs.reshape(g.shape) * w.astype(jnp.float32)


def causal_conv_silu(xbc, conv_state, w, b):
    seq = xbc.shape[1]
    xp = jnp.concatenate([conv_state.astype(xbc.dtype), xbc], axis=1)
    out = b + xp[:, 0:seq] * w[0]
    for tap in range(1, CONV_WIDTH):
        out = out + xp[:, tap:tap + seq] * w[tap]
    return jax.nn.silu(out), xp[:, seq:]


def ssd_scan(x, dt, a, bmat, cmat, h0):
    bsz, seq, nh, hp = x.shape
    q = min(SSD_CHUNK, seq)
    nc = seq // q
    rep = nh // SSM_GROUPS
    bmat = jnp.repeat(bmat, rep, axis=2)
    cmat = jnp.repeat(cmat, rep, axis=2)
    chunk = lambda t: t.reshape((bsz, nc, q) + t.shape[2:])
    x, dt, bmat, cmat = chunk(x), chunk(dt), chunk(bmat), chunk(cmat)
    acs = jnp.cumsum(dt * a, axis=2)
    causal = jnp.tril(jnp.ones((q, q), dtype=bool))
    seg = acs[:, :, :, None, :] - acs[:, :, None, :, :]
    decay = jnp.exp(jnp.where(causal[None, None, :, :, None], seg, -jnp.inf))
    scores = jnp.einsum('bcthn,bcshn->bctsh', cmat, bmat) * decay * dt[:, :, None, :, :]
    y_diag = jnp.einsum('bctsh,bcshp->bcthp', scores, x)
    to_end = jnp.exp(acs[:, :, -1:, :] - acs) * dt
    chunk_states = jnp.einsum('bcqhn,bcqhp->bchpn', bmat, x * to_end[..., None])
    chunk_decay = jnp.exp(acs[:, :, -1, :])

    def step(h, inp):
        s_c, d_c = inp
        return h * d_c[:, :, None, None] + s_c, h

    h_final, h_in = lax.scan(step, h0, (jnp.moveaxis(chunk_states, 1, 0),
                                        jnp.moveaxis(chunk_decay, 1, 0)))
    h_in = jnp.moveaxis(h_in, 0, 1)
    y_off = jnp.einsum('bcthn,bchpn->bcthp', cmat, h_in) * jnp.exp(acs)[..., None]
    return (y_diag + y_off).reshape(bsz, seq, nh, hp), h_final


def moba_blocks(k_all, v_all):
    bsz, seq, nh, hd = k_all.shape
    nb = -(-seq // MOBA_BLOCK)
    pad = ((0, 0), (0, nb * MOBA_BLOCK - seq), (0, 0), (0, 0))
    kb = jnp.pad(k_all, pad).reshape(bsz, nb, MOBA_BLOCK, nh, hd)
    vb = jnp.pad(v_all, pad).reshape(bsz, nb, MOBA_BLOCK, nh, hd)
    k_mean = jnp.mean(kb.astype(jnp.float32), axis=2)
    return kb, vb, k_mean


def moba_attend(q, q_pos, kb, vb, k_mean):
    bsz, nq, nh, hd = q.shape
    nb = k_mean.shape[1]
    n_sel = min(MOBA_TOPK, nb)
    own = q_pos // MOBA_BLOCK
    qf = q.astype(jnp.float32)
    gate = jnp.einsum('bqhd,bnhd->bqhn', qf, k_mean)
    fully_past = jnp.arange(nb)[None, :] < own[:, None]
    gate = jnp.where(fully_past[None, :, None, :], gate, -jnp.inf)
    _, sel = lax.top_k(gate, n_sel)
    blocks = jnp.concatenate(
        [sel, jnp.broadcast_to(own[None, :, None, None], (bsz, nq, nh, 1))], axis=-1)
    b_idx = jnp.arange(bsz)[:, None, None, None]
    h_idx = jnp.arange(nh)[None, None, :, None]
    kg = kb[b_idx, blocks, :, h_idx]
    vg = vb[b_idx, blocks, :, h_idx]
    key_pos = blocks[..., None] * MOBA_BLOCK + jnp.arange(MOBA_BLOCK)
    is_own = (jnp.arange(n_sel + 1) == n_sel)[:, None]
    valid = jnp.where(is_own, key_pos <= q_pos[None, :, None, None, None],
                      (blocks < own[None, :, None, None])[..., None])
    s = jnp.einsum('bqhd,bqhjkd->bqhjk', qf, kg.astype(jnp.float32)) * (1.0 / math.sqrt(hd))
    s = jnp.where(valid, s, -jnp.inf).reshape(bsz, nq, nh, (n_sel + 1) * MOBA_BLOCK)
    p = jax.nn.softmax(s, axis=-1).reshape(bsz, nq, nh, n_sel + 1, MOBA_BLOCK)
    o = jnp.einsum('bqhjk,bqhjkd->bqhd', p, vg.astype(jnp.float32))
    return o.astype(q.dtype)


def moe_ffn(h, w_router, b_router, w_gate_up, b_gate_up, w_down, b_down):
    n_tok, dm = h.shape
    logits = (h @ w_router + b_router).astype(jnp.float32)
    top_logit, top_idx = lax.top_k(logits, TOP_K)
    gates = jax.nn.softmax(top_logit, axis=-1)
    n_assign = n_tok * TOP_K
    flat_e = top_idx.reshape(-1)
    flat_tok = jnp.repeat(jnp.arange(n_tok, dtype=jnp.int32), TOP_K)
    order = jnp.argsort(flat_e)
    sorted_e = flat_e[order]
    counts = jnp.bincount(flat_e, length=N_EXPERTS)
    padded = (counts + MOE_BLOCK - 1) // MOE_BLOCK * MOE_BLOCK
    pad_end = jnp.cumsum(padded)
    pad_start = pad_end - padded
    grp_start = jnp.cumsum(counts) - counts
    slot = pad_start[sorted_e] + jnp.arange(n_assign, dtype=jnp.int32) - grp_start[sorted_e]
    n_blocks = -(-n_assign // MOE_BLOCK) + N_EXPERTS
    cap = n_blocks * MOE_BLOCK
    slot_tok = jnp.full((cap,), n_tok, jnp.int32).at[slot].set(flat_tok[order])
    block_expert = jnp.minimum(
        jnp.searchsorted(pad_end, jnp.arange(n_blocks, dtype=jnp.int32) * MOE_BLOCK, side='right'),
        N_EXPERTS - 1)
    h_pad = jnp.concatenate([h, jnp.zeros((1, dm), h.dtype)], axis=0)
    xb = h_pad[slot_tok].reshape(n_blocks, MOE_BLOCK, dm)

    def expert_block(args):
        xblk, e = args
        gu = xblk @ w_gate_up[e] + b_gate_up[e]
        g = jnp.minimum(gu[:, :D_FF], SWIGLU_LIMIT)
        u = jnp.clip(gu[:, D_FF:], -SWIGLU_LIMIT, SWIGLU_LIMIT)
        act = (u + 1.0) * (g * jax.nn.sigmoid(SWIGLU_ALPHA * g))
        return act @ w_down[e] + b_down[e]

    yb = lax.map(expert_block, (xb, block_expert)).reshape(cap, dm)
    slot_of_assign = jnp.zeros((n_assign,), jnp.int32).at[order].set(slot)
    y_assign = yb[slot_of_assign].reshape(n_tok, TOP_K, dm)
    return jnp.einsum('tk,tkd->td', gates.astype(h.dtype), y_assign)


def hybrid_layer(x, pos, q_block, k_past, v_past, conv_state, ssm_state,
                 w_in, conv_w, conv_b, dt_bias, a_log, d_skip, ssm_norm_w, w_out,
                 ln1_w, ln1_b, w_router, b_router, w_gate_up, b_gate_up, w_down, b_down,
                 ln2_w, ln2_b):
    bsz, seq, _ = x.shape
    f32 = jnp.float32
    proj = x @ w_in
    q, k, v, z, xbc, dt_raw = jnp.split(
        proj, [D_ATTN, 2 * D_ATTN, 3 * D_ATTN, 3 * D_ATTN + D_INNER,
               3 * D_ATTN + D_INNER + CONV_CH], axis=-1)
    heads = lambda t: t.reshape(bsz, seq, ATTN_HEADS, ATTN_HEAD_DIM)
    q, k, v = heads(q), heads(k), heads(v)

    k_all = k if k_past is None else jnp.concatenate([k_past.astype(k.dtype), k], axis=1)
    v_all = v if v_past is None else jnp.concatenate([v_past.astype(v.dtype), v], axis=1)
    kb, vb, k_mean = moba_blocks(k_all, v_all)
    nqb = seq // q_block
    qb = q.reshape(bsz, nqb, q_block, ATTN_HEADS, ATTN_HEAD_DIM).transpose(1, 0, 2, 3, 4)
    pb = pos.reshape(nqb, q_block)
    ob = lax.map(lambda qp: moba_attend(qp[0], qp[1], kb, vb, k_mean), (qb, pb))
    attn_out = ob.transpose(1, 0, 2, 3, 4).reshape(bsz, seq, D_ATTN)

    xbc_act, new_conv = causal_conv_silu(xbc, conv_state, conv_w, conv_b)
    xs, bm, cm = jnp.split(xbc_act, [D_INNER, D_INNER + SSM_GROUPS * SSM_STATE], axis=-1)
    xs = xs.reshape(bsz, seq, SSM_HEADS, SSM_HEAD_DIM).astype(f32)
    bm = bm.reshape(bsz, seq, SSM_GROUPS, SSM_STATE).astype(f32)
    cm = cm.reshape(bsz, seq, SSM_GROUPS, SSM_STATE).astype(f32)
    dt = jax.nn.softplus(dt_raw.astype(f32) + dt_bias.astype(f32))
    a = -jnp.exp(a_log.astype(f32))
    y, new_ssm = ssd_scan(xs, dt, a, bm, cm, ssm_state.astype(f32))
    y = y + d_skip.astype(f32)[:, None] * xs
    ssd_out = gated_group_rmsnorm(y.reshape(bsz, seq, D_INNER), z, ssm_norm_w).astype(x.dtype)

    mixed = jnp.concatenate([attn_out, ssd_out], axis=-1) @ w_out
    h1 = layer_norm(DEEPNORM_ALPHA * x + mixed, ln1_w, ln1_b)
    ff = moe_ffn(h1.reshape(bsz * seq, D_MODEL), w_router, b_router, w_gate_up, b_gate_up,
                 w_down, b_down).reshape(bsz, seq, D_MODEL)
    out = layer_norm(DEEPNORM_ALPHA * h1 + ff, ln2_w, ln2_b)
    return out, k, v, new_conv, new_ssm.astype(ssm_state.dtype)


def setup_inputs(seed: int = 0) -> dict:
    key = jax.random.key(seed)
    ks = jax.random.split(key, 26)
    f32 = jnp.float32
    n_pages = PAST_LEN // PAGE_SIZE
    n_used = DEC_BATCH * n_pages
    n_pool = n_used + max(1, n_used // 4)
    nrm = lambda k, shape, scale: scale * jax.random.normal(k, shape, f32)
    x_prompt = nrm(ks[0], (BATCH, SEQ, D_MODEL), 1.0)
    x_sample = nrm(ks[1], (DEC_BATCH, DEC_SEQ, D_MODEL), 1.0)
    cache_k = nrm(ks[2], (DEPTH, n_pool, PAGE_SIZE, ATTN_HEADS, ATTN_HEAD_DIM), 1.0)
    cache_v = nrm(ks[3], (DEPTH, n_pool, PAGE_SIZE, ATTN_HEADS, ATTN_HEAD_DIM), 1.0)
    state_conv = nrm(ks[4], (DEPTH, DEC_BATCH, CONV_WIDTH - 1, CONV_CH), 1.0)
    state_ssm = nrm(ks[5], (DEPTH, DEC_BATCH, SSM_HEADS, SSM_HEAD_DIM, SSM_STATE), 0.1)
    page_table = jax.random.permutation(ks[6], n_pool)[:n_used].reshape(
        DEC_BATCH, n_pages).astype(jnp.int32)
    w_in = nrm(ks[7], (DEPTH, D_MODEL, D_PROJ), D_MODEL ** -0.5)
    conv_w = nrm(ks[8], (DEPTH, CONV_WIDTH, CONV_CH), CONV_WIDTH ** -0.5)
    conv_b = nrm(ks[9], (DEPTH, CONV_CH), 0.01)
    dt0 = jnp.exp(jax.random.uniform(ks[10], (DEPTH, SSM_HEADS), f32,
                                     math.log(1e-3), math.log(1e-1)))
    dt_bias = dt0 + jnp.log(-jnp.expm1(-dt0))
    a_log = jnp.log(jax.random.uniform(ks[11], (DEPTH, SSM_HEADS), f32, 1.0, 16.0))
    d_skip = 1.0 + nrm(ks[12], (DEPTH, SSM_HEADS), 0.1)
    ssm_norm_w = 1.0 + nrm(ks[13], (DEPTH, D_INNER), 0.02)
    w_out = nrm(ks[14], (DEPTH, D_MIX, D_MODEL), DEEPNORM_BETA * D_MIX ** -0.5)
    ln1_w = 1.0 + nrm(ks[15], (DEPTH, D_MODEL), 0.02)
    ln1_b = nrm(ks[16], (DEPTH, D_MODEL), 0.02)
    w_router = nrm(ks[17], (DEPTH, D_MODEL, N_EXPERTS), D_MODEL ** -0.5)
    b_router = nrm(ks[18], (DEPTH, N_EXPERTS), 0.01)
    w_gate_up = nrm(ks[19], (DEPTH, N_EXPERTS, D_MODEL, 2 * D_FF), D_MODEL ** -0.5)
    b_gate_up = nrm(ks[20], (DEPTH, N_EXPERTS, 2 * D_FF), 0.01)
    w_down = nrm(ks[21], (DEPTH, N_EXPERTS, D_FF, D_MODEL), DEEPNORM_BETA * D_FF ** -0.5)
    b_down = nrm(ks[22], (DEPTH, N_EXPERTS, D_MODEL), 0.01)
    ln2_w = 1.0 + nrm(ks[23], (DEPTH, D_MODEL), 0.02)
    ln2_b = nrm(ks[24], (DEPTH, D_MODEL), 0.02)
    return {'x_prompt': x_prompt, 'x_sample': x_sample, 'cache_k': cache_k, 'cache_v': cache_v,
            'state_conv': state_conv, 'state_ssm': state_ssm, 'page_table': page_table,
            'w_in': w_in, 'conv_w': conv_w, 'conv_b': conv_b, 'dt_bias': dt_bias,
            'a_log': a_log, 'd_skip': d_skip, 'ssm_norm_w': ssm_norm_w, 'w_out': w_out,
            'ln1_w': ln1_w, 'ln1_b': ln1_b, 'w_router': w_router, 'b_router': b_router,
            'w_gate_up': w_gate_up, 'b_gate_up': b_gate_up, 'w_down': w_down,
            'b_down': b_down, 'ln2_w': ln2_w, 'ln2_b': ln2_b}


def reference(x_prompt, x_sample, cache_k, cache_v, state_conv, state_ssm, page_table,
              w_in, conv_w, conv_b, dt_bias, a_log, d_skip, ssm_norm_w, w_out,
              ln1_w, ln1_b, w_router, b_router, w_gate_up, b_gate_up, w_down, b_down,
              ln2_w, ln2_b):
    n_pages = PAST_LEN // PAGE_SIZE
    pos_prompt = jnp.arange(SEQ, dtype=jnp.int32)
    pos_sample = PAST_LEN + jnp.arange(DEC_SEQ, dtype=jnp.int32)
    hp, hs = x_prompt, x_sample
    kp_l, vp_l, cp_l, sp_l = [], [], [], []
    ks_l, vs_l, cs_l, ss_l = [], [], [], []
    for l in range(DEPTH):
        lw = (w_in[l], conv_w[l], conv_b[l], dt_bias[l], a_log[l], d_skip[l], ssm_norm_w[l],
              w_out[l], ln1_w[l], ln1_b[l], w_router[l], b_router[l], w_gate_up[l],
              b_gate_up[l], w_down[l], b_down[l], ln2_w[l], ln2_b[l])
        conv0 = jnp.zeros((BATCH, CONV_WIDTH - 1, CONV_CH), hp.dtype)
        ssm0 = jnp.zeros((BATCH, SSM_HEADS, SSM_HEAD_DIM, SSM_STATE), state_ssm.dtype)
        hp, kp, vp, cp, sp = hybrid_layer(hp, pos_prompt, ATTN_Q_BLOCK, None, None,
                                          conv0, ssm0, *lw)
        k_past = cache_k[l][page_table].reshape(DEC_BATCH, n_pages * PAGE_SIZE,
                                                ATTN_HEADS, ATTN_HEAD_DIM)
        v_past = cache_v[l][page_table].reshape(DEC_BATCH, n_pages * PAGE_SIZE,
                                                ATTN_HEADS, ATTN_HEAD_DIM)
        hs, ks_, vs_, cs_, ss_ = hybrid_layer(hs, pos_sample, DEC_SEQ, k_past, v_past,
                                              state_conv[l], state_ssm[l], *lw)
        kp_l.append(kp); vp_l.append(vp); cp_l.append(cp); sp_l.append(sp)
        ks_l.append(ks_); vs_l.append(vs_); cs_l.append(cs_); ss_l.append(ss_)
    return (hp, hs, jnp.stack(kp_l), jnp.stack(vp_l), jnp.stack(cp_l), jnp.stack(sp_l),
            jnp.stack(ks_l), jnp.stack(vs_l), jnp.stack(cs_l), jnp.stack(ss_l))
```

```python
import functools
import math

import jax
import jax.numpy as jnp
from jax import lax
from jax.experimental import pallas as pl
from jax.experimental.pallas import tpu as pltpu

F32 = jnp.float32
BF16 = jnp.bfloat16
HIGHEST = lax.Precision.HIGHEST

D_MODEL = 2048
N_HEADS = 8
HEAD_DIM = 128
D_ATTN = N_HEADS * HEAD_DIM
MOBA_BLOCK = 256
MOBA_TOPK = 3
D_INNER = 1024
SSM_HEAD_DIM = 64
SSM_HEADS = 16
SSM_GROUPS = 2
SSM_STATE = 128
CONV_WIDTH = 4
CONV_CH = D_INNER + 2 * SSM_GROUPS * SSM_STATE
SSD_CHUNK = 128
N_EXPERTS = 32
TOP_K = 4
D_FF = 2048
SWIGLU_LIMIT = 7.0
SWIGLU_ALPHA = 1.702
NORM_EPS = 1e-5
DEEPNORM_ALPHA = 2.0 ** 0.25
PAGE_SIZE = 128

LANES = 128
SUBLANES = 8
VMEM_LIMIT_BYTES = 56 * 1024 * 1024

DT_PAD = LANES
D_PROJ_MAIN = 3 * D_ATTN + D_INNER + CONV_CH
NEG_BIG = -1e30


def _params(*sem):
    return pltpu.CompilerParams(dimension_semantics=sem, vmem_limit_bytes=VMEM_LIMIT_BYTES)


def _nt(a, b, precision=None):
    return lax.dot_general(a, b, (((1,), (1,)), ((), ())), precision=precision,
                           preferred_element_type=F32)


def _mm(a, b, precision=None):
    return jnp.dot(a, b, precision=precision, preferred_element_type=F32)


def _sigmoid(x):
    return 1.0 / (1.0 + jnp.exp(-x))


def _in_proj_kernel(x_ref, w_ref, q_ref, k_ref, v_ref, z_ref, xbc_ref, dt_ref):
    xb = x_ref[...].astype(BF16)
    col = 0
    for ref in (q_ref, k_ref, v_ref, z_ref, xbc_ref, dt_ref):
        width = ref.shape[1]
        ref[...] = _mm(xb, w_ref[:, col:col + width])
        col += width


def _in_proj(x, w_bf16, tm):
    t = x.shape[0]
    widths = (D_ATTN, D_ATTN, D_ATTN, D_INNER, CONV_CH, DT_PAD)
    return pl.pallas_call(
        _in_proj_kernel,
        grid=(t // tm,),
        in_specs=[pl.BlockSpec((tm, D_MODEL), lambda i: (i, 0)),
                  pl.BlockSpec(w_bf16.shape, lambda i: (0, 0), pipeline_mode=pl.Buffered(1))],
        out_specs=[pl.BlockSpec((tm, w), lambda i: (i, 0)) for w in widths],
        out_shape=[jax.ShapeDtypeStruct((t, w), F32) for w in widths],
        compiler_params=_params("parallel"),
        name="in_proj",
    )(x, w_bf16)


def _moba_prompt_kernel(q_ref, k_ref, v_ref, o_ref):
    seq = q_ref.shape[0]
    nb = seq // MOBA_BLOCK
    scale = 1.0 / math.sqrt(HEAD_DIM)
    q = q_ref[...]
    k = k_ref[...]
    qb = q.astype(BF16)
    kb = k.astype(BF16)
    vb = v_ref[...].astype(BF16)
    kmean = jnp.sum(k.reshape(nb, MOBA_BLOCK, HEAD_DIM), axis=1) * (1.0 / MOBA_BLOCK)
    kmean = jnp.concatenate([kmean, jnp.zeros((LANES - nb, HEAD_DIM), F32)], axis=0)
    gate = _nt(q, kmean, HIGHEST)
    row = lax.broadcasted_iota(jnp.int32, (MOBA_BLOCK, MOBA_BLOCK), 0)
    colid = lax.broadcasted_iota(jnp.int32, (MOBA_BLOCK, MOBA_BLOCK), 1)
    causal = colid <= row
    for i in range(nb):
        rows = slice(i * MOBA_BLOCK, (i + 1) * MOBA_BLOCK)
        g = [gate[rows, j:j + 1] for j in range(i)]
        s_blocks = []
        for j in range(i + 1):
            s = _nt(qb[rows], kb[j * MOBA_BLOCK:(j + 1) * MOBA_BLOCK]) * scale
            if j == i:
                s = jnp.where(causal, s, NEG_BIG)
            else:
                rank = jnp.zeros((MOBA_BLOCK, 1), jnp.int32)
                for jj in range(i):
                    if jj == j:
                        continue
                    ahead = (g[jj] > g[j]) | ((g[jj] == g[j]) & (jj < j))
                    rank = rank + ahead.astype(jnp.int32)
                s = jnp.where(rank < MOBA_TOPK, s, NEG_BIG)
            s_blocks.append(s)
        m = s_blocks[0].max(axis=1, keepdims=True)
        for s in s_blocks[1:]:
            m = jnp.maximum(m, s.max(axis=1, keepdims=True))
        l = jnp.zeros((MOBA_BLOCK, 1), F32)
        o = jnp.zeros((MOBA_BLOCK, HEAD_DIM), F32)
        for j, s in enumerate(s_blocks):
            p = jnp.exp(s - m)
            l = l + p.sum(axis=1, keepdims=True)
            o = o + _mm(p.astype(BF16), vb[j * MOBA_BLOCK:(j + 1) * MOBA_BLOCK])
        o_ref[rows, :] = o / l


def _moba_prompt(q, k, v, bsz, seq):
    spec = pl.BlockSpec((seq, HEAD_DIM), lambda b, h: (b, h))
    return pl.pallas_call(
        _moba_prompt_kernel,
        grid=(bsz, N_HEADS),
        in_specs=[spec, spec, spec],
        out_specs=spec,
        out_shape=jax.ShapeDtypeStruct((bsz * seq, D_ATTN), F32),
        compiler_params=_params("parallel", "parallel"),
        name="moba_prompt",
    )(q, k, v)


N_QH = 32


def _moba_sample_kernel(pt_ref, qt_ref, kn_ref, vn_ref, ck_ref, cv_ref, o_ref,
                        wt_ref, wtb_ref, st_ref, ksum_ref, p_ref, acc_ref, l_ref, *, n_new):
    del pt_ref
    ph = pl.program_id(1)
    p = pl.program_id(2)
    n_pages = pl.num_programs(2)
    n_blocks = ksum_ref.shape[0]
    pages_per_block = MOBA_BLOCK // PAGE_SIZE
    scale = 1.0 / math.sqrt(HEAD_DIM)

    @pl.when((ph == 0) & (p == 0))
    def _():
        lane_head = lax.broadcasted_iota(jnp.int32, (N_QH, D_ATTN), 1) // HEAD_DIM
        row_head = lax.broadcasted_iota(jnp.int32, (N_QH, D_ATTN), 0) % N_HEADS
        wt = jnp.where(lane_head == row_head, qt_ref[0], 0.0)
        wt_ref[...] = wt
        wtb_ref[...] = wt.astype(BF16)
        ksum_ref[...] = jnp.zeros_like(ksum_ref)

    @pl.when(ph == 0)
    def _():
        kp = ck_ref[0]
        st_ref[p] = _nt(wtb_ref[...], kp.astype(BF16)) * scale
        blk = p // pages_per_block
        ksum_ref[pl.ds(blk, 1), :] += jnp.sum(kp, axis=0, keepdims=True)

    @pl.when((ph == 1) & (p == 0))
    def _():
        kmean = ksum_ref[...] * (1.0 / MOBA_BLOCK)
        gate = _nt(wt_ref[...], kmean, HIGHEST)
        blk_id = lax.broadcasted_iota(jnp.int32, gate.shape, 1)
        rank = jnp.zeros(gate.shape, jnp.int32)
        for jj in range(n_blocks):
            gj = gate[:, jj:jj + 1]
            ahead = (gj > gate) | ((gj == gate) & (jj < blk_id))
            rank = rank + ahead.astype(jnp.int32)
        sel = rank < MOBA_TOPK
        m = jnp.full((N_QH, 1), NEG_BIG, F32)
        for pg in range(n_blocks * pages_per_block):
            j = pg // pages_per_block
            s = jnp.where(sel[:, j:j + 1], st_ref[pg], NEG_BIG)
            st_ref[pg] = s
            m = jnp.maximum(m, s.max(axis=1, keepdims=True))
        q_idx = lax.broadcasted_iota(jnp.int32, (N_QH, 1), 0) // N_HEADS
        wt = wt_ref[...]
        s_new = []
        for kk in range(n_new):
            s = jnp.sum(wt * kn_ref[0, kk:kk + 1, :], axis=1, keepdims=True) * scale
            s = jnp.where(kk <= q_idx, s, NEG_BIG)
            s_new.append(s)
            m = jnp.maximum(m, s)
        l = jnp.zeros((N_QH, 1), F32)
        acc = jnp.zeros((N_QH, D_ATTN), F32)
        for kk in range(n_new):
            pn = jnp.exp(s_new[kk] - m)
            l = l + pn
            acc = acc + pn * vn_ref[0, kk:kk + 1, :]
        for pg in range(n_blocks * pages_per_block):
            pj = jnp.exp(st_ref[pg] - m)
            l = l + pj.sum(axis=1, keepdims=True)
            p_ref[pg] = pj.astype(BF16)
        l_ref[...] = l
        acc_ref[...] = acc

    @pl.when(ph == 1)
    def _():
        acc_ref[...] += _mm(p_ref[p], cv_ref[0].astype(BF16))

    @pl.when((ph == 1) & (p == n_pages - 1))
    def _():
        lane_head = lax.broadcasted_iota(jnp.int32, (N_QH, D_ATTN), 1) // HEAD_DIM
        row_head = lax.broadcasted_iota(jnp.int32, (N_QH, D_ATTN), 0) % N_HEADS
        o = jnp.where(lane_head == row_head, acc_ref[...] / l_ref[...], 0.0)
        o_ref[0] = jnp.sum(o.reshape(N_QH // N_HEADS, N_HEADS, D_ATTN), axis=1)[:n_new]


def _moba_sample(q, k_new, v_new, cache_k, cache_v, page_table):
    bsz, n_new, _ = q.shape
    n_pages = page_table.shape[1]
    n_blocks = n_pages * PAGE_SIZE // MOBA_BLOCK
    assert n_new * N_HEADS <= N_QH and n_new <= SUBLANES
    pad = ((0, 0), (0, N_QH // N_HEADS - n_new), (0, 0))
    qt = jnp.repeat(jnp.pad(q, pad), N_HEADS, axis=1)
    kn = jnp.pad(k_new, ((0, 0), (0, SUBLANES - n_new), (0, 0)))
    vn = jnp.pad(v_new, ((0, 0), (0, SUBLANES - n_new), (0, 0)))
    pt = page_table.reshape(-1)

    def k_map(b, ph, p, pt):
        return (jnp.where(ph == 0, pt[b * n_pages + p], pt[b * n_pages + n_pages - 1]), 0, 0)

    def v_map(b, ph, p, pt):
        return (jnp.where(ph == 1, pt[b * n_pages + p], pt[b * n_pages]), 0, 0)

    small = lambda rows: pl.BlockSpec((1, rows, D_ATTN), lambda b, ph, p, pt: (b, 0, 0))
    grid_spec = pltpu.PrefetchScalarGridSpec(
        num_scalar_prefetch=1,
        grid=(bsz, 2, n_pages),
        in_specs=[small(N_QH), small(SUBLANES), small(SUBLANES),
                  pl.BlockSpec((1, PAGE_SIZE, D_ATTN), k_map),
                  pl.BlockSpec((1, PAGE_SIZE, D_ATTN), v_map)],
        out_specs=small(n_new),
        scratch_shapes=[pltpu.VMEM((N_QH, D_ATTN), F32),
                        pltpu.VMEM((N_QH, D_ATTN), BF16),
                        pltpu.VMEM((n_pages, N_QH, PAGE_SIZE), F32),
                        pltpu.VMEM((n_blocks, D_ATTN), F32),
                        pltpu.VMEM((n_pages, N_QH, PAGE_SIZE), BF16),
                        pltpu.VMEM((N_QH, D_ATTN), F32),
                        pltpu.VMEM((N_QH, 1), F32)])
    return pl.pallas_call(
        functools.partial(_moba_sample_kernel, n_new=n_new),
        grid_spec=grid_spec,
        out_shape=jax.ShapeDtypeStruct((bsz, n_new, D_ATTN), F32),
        compiler_params=_params("parallel", "arbitrary", "arbitrary"),
        name="moba_sample",
    )(pt, qt, kn, vn, cache_k, cache_v)


def _ssd_kernel(xbc_ref, z_ref, dtr_ref, conv0_ref, h0_ref, cw_ref, cb_ref, dtb_ref, alog_ref,
                dskip_ref, nw_ref, y_ref, hfin_ref, tail_ref, xx_ref, st_ref, yd_ref, *, valid_len):
    c = pl.program_id(1)
    nc = pl.num_programs(1)
    q = xbc_ref.shape[0]
    hp = SSM_HEAD_DIM
    gw = D_INNER // SSM_GROUPS
    heads_per_group = SSM_HEADS // SSM_GROUPS

    @pl.when(c == 0)
    def _():
        tail_ref[...] = conv0_ref[0]
        st_ref[...] = h0_ref[0].T

    xbc = xbc_ref[...]
    xx_ref[0:SUBLANES, :] = tail_ref[...]
    xx_ref[SUBLANES:SUBLANES + q, :] = xbc
    tail_ref[...] = xbc[q - SUBLANES:q, :]
    acc = cb_ref[...] + xx_ref[pl.ds(SUBLANES - CONV_WIDTH + 1, q), :] * cw_ref[0:1, :]
    for tap in range(1, CONV_WIDTH):
        acc = acc + xx_ref[pl.ds(SUBLANES - CONV_WIDTH + 1 + tap, q), :] * cw_ref[tap:tap + 1, :]
    xact = acc * _sigmoid(acc)
    xs = xact[:, :D_INNER]
    bm = xact[:, D_INNER:D_INNER + SSM_GROUPS * SSM_STATE]
    cm = xact[:, D_INNER + SSM_GROUPS * SSM_STATE:]

    v = dtr_ref[...] + dtb_ref[...]
    dt = jnp.maximum(v, 0.0) + jnp.log1p(jnp.exp(-jnp.abs(v)))
    if valid_len is not None:
        t_idx = c * q + lax.broadcasted_iota(jnp.int32, dt.shape, 0)
        dt = jnp.where(t_idx < valid_len, dt, 0.0)
    a = -jnp.exp(alog_ref[...])
    r_i = lax.broadcasted_iota(jnp.int32, (q, q), 0)
    c_i = lax.broadcasted_iota(jnp.int32, (q, q), 1)
    tril = c_i <= r_i
    acs = _mm(tril.astype(F32), dt * a, HIGHEST)
    acs_last = acs[q - 1:q, :]
    acs_t = acs.T
    e_r = lax.broadcasted_iota(jnp.int32, (LANES, D_INNER), 0)
    e_c = lax.broadcasted_iota(jnp.int32, (LANES, D_INNER), 1)
    expand = (e_c // hp == e_r).astype(F32)
    dt_e = _mm(dt, expand, HIGHEST)
    to_end_e = _mm(jnp.exp(acs_last - acs) * dt, expand, HIGHEST)
    eacs_e = _mm(jnp.exp(acs), expand, HIGHEST)
    cdec_e = _mm(jnp.exp(acs_last), expand, HIGHEST)

    xdt_b = (xs * dt_e).astype(BF16)
    xw_b = (xs * to_end_e).astype(BF16)
    for g in range(SSM_GROUPS):
        bg = bm[:, g * SSM_STATE:(g + 1) * SSM_STATE]
        cg_b = cm[:, g * SSM_STATE:(g + 1) * SSM_STATE].astype(BF16)
        cb = _nt(cg_b, bg.astype(BF16))
        for hh in range(heads_per_group):
            h = g * heads_per_group + hh
            seg = acs[:, h:h + 1] - acs_t[h:h + 1, :]
            decay = jnp.where(tril, jnp.exp(jnp.minimum(seg, 0.0)), 0.0)
            yd_ref[:, h * hp:(h + 1) * hp] = _mm((cb * decay).astype(BF16),
                                                 xdt_b[:, h * hp:(h + 1) * hp])
        gc = slice(g * gw, (g + 1) * gw)
        st_g = st_ref[:, gc]
        y_off = _mm(cg_b, st_g.astype(BF16)) * eacs_e[:, gc]
        yd_ref[:, gc] += y_off
        st_ref[:, gc] = st_g * cdec_e[:, gc] + _mm(bg.T.astype(BF16), xw_b[:, gc])

    y = yd_ref[...] + dskip_ref[...] * xs
    zz = z_ref[...]
    gated = y * (zz * _sigmoid(zz))
    for g in range(SSM_GROUPS):
        gc = slice(g * gw, (g + 1) * gw)
        gg = gated[:, gc]
        ms = jnp.mean(gg * gg, axis=1, keepdims=True)
        y_ref[:, gc] = gg * lax.rsqrt(ms + NORM_EPS) * nw_ref[:, gc]

    @pl.when(c == nc - 1)
    def _():
        hfin_ref[0] = st_ref[...].T


def _ssd(xbc, z, dtr, conv0, h0, conv_w, conv_b, dt_bias, a_log, d_skip, norm_w, bsz, seq,
         valid_len=None):
    q = min(SSD_CHUNK, seq)
    nc = seq // q
    pad16 = lambda t: jnp.pad(t.reshape(1, SSM_HEADS), ((0, 0), (0, DT_PAD - SSM_HEADS)))
    row = lambda w: pl.BlockSpec((q, w), lambda b, c: (b * nc + c, 0))
    const = lambda shape: pl.BlockSpec(shape, lambda b, c: (0,) * len(shape))
    per_b = lambda shape: pl.BlockSpec((1,) + shape, lambda b, c: (b, 0, 0))
    return pl.pallas_call(
        functools.partial(_ssd_kernel, valid_len=valid_len),
        grid=(bsz, nc),
        in_specs=[row(CONV_CH), row(D_INNER), row(DT_PAD),
                  per_b((SUBLANES, CONV_CH)), per_b((D_INNER, SSM_STATE)),
                  const((CONV_WIDTH, CONV_CH)), const((1, CONV_CH)),
                  const((1, DT_PAD)), const((1, DT_PAD)), const((1, D_INNER)),
                  const((1, D_INNER))],
        out_specs=[row(D_INNER), per_b((D_INNER, SSM_STATE))],
        out_shape=[jax.ShapeDtypeStruct((bsz * seq, D_INNER), F32),
                   jax.ShapeDtypeStruct((bsz, D_INNER, SSM_STATE), F32)],
        scratch_shapes=[pltpu.VMEM((SUBLANES, CONV_CH), F32),
                        pltpu.VMEM((q + SUBLANES, CONV_CH), F32),
                        pltpu.VMEM((SSM_STATE, D_INNER), F32),
                        pltpu.VMEM((q, D_INNER), F32)],
        compiler_params=_params("parallel", "arbitrary"),
        name="ssd",
    )(xbc, z, dtr, conv0, h0, conv_w, conv_b.reshape(1, CONV_CH), pad16(dt_bias), pad16(a_log),
      jnp.repeat(d_skip, SSM_HEAD_DIM).reshape(1, D_INNER), norm_w.reshape(1, D_INNER))


def _layer_norm(x, w, b):
    mu = jnp.mean(x, axis=-1, keepdims=True)
    xc = x - mu
    var = jnp.mean(xc * xc, axis=-1, keepdims=True)
    return xc * lax.rsqrt(var + NORM_EPS) * w + b


def _out_proj_kernel(attn_ref, ssd_ref, x_ref, w_ref, lnw_ref, lnb_ref, wr_ref, br_ref,
                     h_ref, hb_ref, logit_ref):
    mixed = (_mm(attn_ref[...].astype(BF16), w_ref[0:D_ATTN, :])
             + _mm(ssd_ref[...].astype(BF16), w_ref[D_ATTN:, :]))
    h = _layer_norm(DEEPNORM_ALPHA * x_ref[...] + mixed, lnw_ref[...], lnb_ref[...])
    h_ref[...] = h
    hb_ref[...] = h.astype(BF16)
    logit_ref[...] = _mm(h, wr_ref[...], HIGHEST) + br_ref[...]


def _out_proj(attn, ssd, x, w_bf16, ln_w, ln_b, w_router_pad, b_router_pad, tm):
    t = x.shape[0]
    row = lambda w: pl.BlockSpec((tm, w), lambda i: (i, 0))
    const = lambda shape: pl.BlockSpec(shape, lambda i: (0, 0))
    return pl.pallas_call(
        _out_proj_kernel,
        grid=(t // tm,),
        in_specs=[row(D_ATTN), row(D_INNER), row(D_MODEL), const((D_MODEL, D_MODEL)),
                  const((1, D_MODEL)), const((1, D_MODEL)), const((D_MODEL, LANES)),
                  const((1, LANES))],
        out_specs=[row(D_MODEL), row(D_MODEL), row(LANES)],
        out_shape=[jax.ShapeDtypeStruct((t, D_MODEL), F32),
                   jax.ShapeDtypeStruct((t, D_MODEL), BF16),
                   jax.ShapeDtypeStruct((t, LANES), F32)],
        compiler_params=_params("parallel"),
        name="out_proj",
    )(attn, ssd, x, w_bf16, ln_w.reshape(1, D_MODEL), ln_b.reshape(1, D_MODEL),
      w_router_pad, b_router_pad)


MOE_BM = 256
MOE_TF = 1024


def _gate_up_kernel(e_ref, n_ref, r_ref, first_ref, valid_ref, x_ref, wg_ref, wu_ref, bg_ref,
                    bu_ref, act_ref, wgb_ref, wub_ref):
    i = pl.program_id(0)

    @pl.when(first_ref[i] == 1)
    def _():
        wgb_ref[...] = wg_ref[0].astype(BF16)
        wub_ref[...] = wu_ref[0].astype(BF16)

    @pl.when(valid_ref[i] == 1)
    def _():
        x = x_ref[...]
        g = jnp.minimum(_mm(x, wgb_ref[...]) + bg_ref[0], SWIGLU_LIMIT)
        u = jnp.clip(_mm(x, wub_ref[...]) + bu_ref[0], -SWIGLU_LIMIT, SWIGLU_LIMIT)
        act_ref[...] = ((u + 1.0) * (g * _sigmoid(SWIGLU_ALPHA * g))).astype(BF16)

    @pl.when(valid_ref[i] == 0)
    def _():
        act_ref[...] = jnp.zeros_like(act_ref)


def _down_kernel(e_ref, r_ref, first_ref, valid_ref, a_ref, wd_ref, bd_ref, y_ref, wdb_ref):
    i = pl.program_id(0)

    @pl.when(first_ref[i] == 1)
    def _():
        wdb_ref[...] = wd_ref[0].astype(BF16)

    @pl.when(valid_ref[i] == 1)
    def _():
        y_ref[...] = _mm(a_ref[...], wdb_ref[...]) + bd_ref[0]

    @pl.when(valid_ref[i] == 0)
    def _():
        y_ref[...] = jnp.zeros_like(y_ref)


def _moe_experts(xs, sched, w_gate_up, b_gate_up, w_down, b_down):
    cap = xs.shape[0]
    n_blocks = cap // MOE_BM
    nt = D_FF // MOE_TF
    up_off = D_FF // MOE_TF
    b_gu = b_gate_up.reshape(N_EXPERTS, 1, 2 * D_FF)
    gu_spec = pltpu.PrefetchScalarGridSpec(
        num_scalar_prefetch=5,
        grid=(n_blocks * nt,),
        in_specs=[
            pl.BlockSpec((MOE_BM, D_MODEL), lambda i, e, n, r, f, v: (r[i], 0)),
            pl.BlockSpec((1, D_MODEL, MOE_TF), lambda i, e, n, r, f, v: (e[i], 0, n[i])),
            pl.BlockSpec((1, D_MODEL, MOE_TF), lambda i, e, n, r, f, v: (e[i], 0, up_off + n[i])),
            pl.BlockSpec((1, 1, MOE_TF), lambda i, e, n, r, f, v: (e[i], 0, n[i])),
            pl.BlockSpec((1, 1, MOE_TF), lambda i, e, n, r, f, v: (e[i], 0, up_off + n[i])),
        ],
        out_specs=pl.BlockSpec((MOE_BM, MOE_TF), lambda i, e, n, r, f, v: (r[i], n[i])),
        scratch_shapes=[pltpu.VMEM((D_MODEL, MOE_TF), BF16), pltpu.VMEM((D_MODEL, MOE_TF), BF16)])
    act = pl.pallas_call(
        _gate_up_kernel,
        grid_spec=gu_spec,
        out_shape=jax.ShapeDtypeStruct((cap, D_FF), BF16),
        compiler_params=_params("arbitrary"),
        name="moe_gate_up",
    )(sched["gu_e"], sched["gu_n"], sched["gu_r"], sched["gu_first"], sched["gu_valid"],
      xs, w_gate_up, w_gate_up, b_gu, b_gu)

    down_spec = pltpu.PrefetchScalarGridSpec(
        num_scalar_prefetch=4,
        grid=(n_blocks,),
        in_specs=[
            pl.BlockSpec((MOE_BM, D_FF), lambda i, e, r, f, v: (r[i], 0)),
            pl.BlockSpec((1, D_FF, D_MODEL), lambda i, e, r, f, v: (e[i], 0, 0)),
            pl.BlockSpec((1, 1, D_MODEL), lambda i, e, r, f, v: (e[i], 0, 0)),
        ],
        out_specs=pl.BlockSpec((MOE_BM, D_MODEL), lambda i, e, r, f, v: (r[i], 0)),
        scratch_shapes=[pltpu.VMEM((D_FF, D_MODEL), BF16)])
    return pl.pallas_call(
        _down_kernel,
        grid_spec=down_spec,
        out_shape=jax.ShapeDtypeStruct((cap, D_MODEL), F32),
        compiler_params=_params("arbitrary"),
        name="moe_down",
    )(sched["d_e"], sched["d_r"], sched["d_first"], sched["d_valid"],
      act, w_down, b_down.reshape(N_EXPERTS, 1, D_MODEL))


def _moe_schedule(logits):
    n_tok = logits.shape[0]
    i32 = jnp.int32
    top_logit, top_idx = lax.top_k(logits, TOP_K)
    gates = jax.nn.softmax(top_logit, axis=-1)
    n_assign = n_tok * TOP_K
    flat_e = top_idx.reshape(-1).astype(i32)
    flat_tok = jnp.repeat(jnp.arange(n_tok, dtype=i32), TOP_K)
    order = jnp.argsort(flat_e).astype(i32)
    sorted_e = flat_e[order]
    counts = jnp.bincount(flat_e, length=N_EXPERTS).astype(i32)
    blocks_e = (counts + MOE_BM - 1) // MOE_BM
    blk_end = jnp.cumsum(blocks_e).astype(i32)
    blk_start = blk_end - blocks_e
    grp_start = (jnp.cumsum(counts) - counts).astype(i32)
    slot = blk_start[sorted_e] * MOE_BM + jnp.arange(n_assign, dtype=i32) - grp_start[sorted_e]
    n_blocks = -(-n_assign // MOE_BM) + N_EXPERTS
    cap = n_blocks * MOE_BM
    slot_tok = jnp.zeros((cap,), i32).at[slot].set(flat_tok[order])
    slot_of_assign = jnp.zeros((n_assign,), i32).at[order].set(slot)
    blk = jnp.arange(n_blocks, dtype=i32)
    blk_e = jnp.minimum(jnp.searchsorted(blk_end, blk, side="right").astype(i32), N_EXPERTS - 1)
    blk_valid = (blk < blk_end[-1]).astype(i32)
    nblk_e = blocks_e.at[N_EXPERTS - 1].add(n_blocks - blk_end[-1])
    nt = D_FF // MOE_TF
    item = jnp.arange(n_blocks * nt, dtype=i32)
    item_end = jnp.cumsum(nblk_e * nt).astype(i32)
    it_e = jnp.minimum(jnp.searchsorted(item_end, item, side="right").astype(i32), N_EXPERTS - 1)
    local = item - blk_start[it_e] * nt
    it_n = local // nblk_e[it_e]
    it_lr = local % nblk_e[it_e]
    it_r = blk_start[it_e] + it_lr
    sched = {"gu_e": it_e, "gu_n": it_n, "gu_r": it_r, "gu_first": (it_lr == 0).astype(i32),
             "gu_valid": blk_valid[it_r],
             "d_e": blk_e, "d_r": blk, "d_first": (blk == blk_start[blk_e]).astype(i32),
             "d_valid": blk_valid}
    return gates, slot_tok, slot_of_assign, sched


def _combine_kernel(y_ref, g_ref, h_ref, lnw_ref, lnb_ref, o_ref):
    g = g_ref[...]
    ff = y_ref[:, 0:D_MODEL] * g[:, 0:1]
    for kk in range(1, TOP_K):
        ff = ff + y_ref[:, kk * D_MODEL:(kk + 1) * D_MODEL] * g[:, kk:kk + 1]
    o_ref[...] = _layer_norm(DEEPNORM_ALPHA * h_ref[...] + ff, lnw_ref[...], lnb_ref[...])


def _combine(y_assign, gates_pad, h, ln_w, ln_b, tm):
    t = h.shape[0]
    row = lambda w: pl.BlockSpec((tm, w), lambda i: (i, 0))
    const = lambda shape: pl.BlockSpec(shape, lambda i: (0, 0))
    return pl.pallas_call(
        _combine_kernel,
        grid=(t // tm,),
        in_specs=[row(TOP_K * D_MODEL), row(LANES), row(D_MODEL), const((1, D_MODEL)),
                  const((1, D_MODEL))],
        out_specs=row(D_MODEL),
        out_shape=jax.ShapeDtypeStruct((t, D_MODEL), F32),
        compiler_params=_params("parallel"),
        name="moe_combine",
    )(y_assign, gates_pad, h, ln_w.reshape(1, D_MODEL), ln_b.reshape(1, D_MODEL))


def _mixer(x, w_in_b, w_out_b, lw, attend, conv0, h0, bsz, seq, tm):
    q, k, v, z, xbc, dtr = _in_proj(x, w_in_b, tm)
    attn = attend(q, k, v)
    if seq % SSD_CHUNK == 0:
        ssd, h_fin = _ssd(xbc, z, dtr, conv0, h0, lw["conv_w"], lw["conv_b"], lw["dt_bias"],
                          lw["a_log"], lw["d_skip"], lw["ssm_norm_w"], bsz, seq)
    else:
        assert seq < SSD_CHUNK
        pad = lambda t: jnp.pad(t.reshape(bsz, seq, -1),
                                ((0, 0), (0, SSD_CHUNK - seq), (0, 0))).reshape(bsz * SSD_CHUNK, -1)
        ssd, h_fin = _ssd(pad(xbc), pad(z), pad(dtr), conv0, h0, lw["conv_w"], lw["conv_b"],
                          lw["dt_bias"], lw["a_log"], lw["d_skip"], lw["ssm_norm_w"], bsz,
                          SSD_CHUNK, valid_len=seq)
        ssd = ssd.reshape(bsz, SSD_CHUNK, D_INNER)[:, :seq].reshape(bsz * seq, D_INNER)
    h1, h1b, logits = _out_proj(attn, ssd, x, w_out_b, lw["ln1_w"], lw["ln1_b"],
                                lw["w_router_pad"], lw["b_router_pad"], tm)
    return h1, h1b, logits, k, v, xbc, h_fin


def kernel(x_prompt, x_sample, cache_k, cache_v, state_conv, state_ssm, page_table, w_in, conv_w,
           conv_b, dt_bias, a_log, d_skip, ssm_norm_w, w_out, ln1_w, ln1_b, w_router, b_router,
           w_gate_up, b_gate_up, w_down, b_down, ln2_w, ln2_b):
    depth = w_in.shape[0]
    assert depth == 1
    bsz, seq, _ = x_prompt.shape
    dbs, dseq, _ = x_sample.shape
    n_pool = cache_k.shape[1]
    tail = CONV_WIDTH - 1
    lw = {"conv_w": conv_w[0], "conv_b": conv_b[0], "dt_bias": dt_bias[0], "a_log": a_log[0],
          "d_skip": d_skip[0], "ssm_norm_w": ssm_norm_w[0], "ln1_w": ln1_w[0], "ln1_b": ln1_b[0],
          "w_router_pad": jnp.pad(w_router[0], ((0, 0), (0, LANES - N_EXPERTS))),
          "b_router_pad": jnp.pad(b_router[0], (0, LANES - N_EXPERTS)).reshape(1, LANES)}
    w_in_b = jnp.pad(w_in[0], ((0, 0), (0, DT_PAD - SSM_HEADS))).astype(BF16)
    w_out_b = w_out[0].astype(BF16)

    xp = x_prompt.reshape(bsz * seq, D_MODEL)
    conv0_p = jnp.zeros((bsz, SUBLANES, CONV_CH), F32)
    h0_p = jnp.zeros((bsz, D_INNER, SSM_STATE), F32)
    h1p, h1pb, logit_p, kp, vp, xbc_p, hfin_p = _mixer(
        xp, w_in_b, w_out_b, lw, lambda q, k, v: _moba_prompt(q, k, v, bsz, seq),
        conv0_p, h0_p, bsz, seq, tm=256)

    n_s = dbs * dseq
    conv0_s = jnp.pad(state_conv[0], ((0, 0), (SUBLANES - tail, 0), (0, 0)))
    h0_s = state_ssm[0].reshape(dbs, D_INNER, SSM_STATE)
    ck = cache_k[0].reshape(n_pool, PAGE_SIZE, D_ATTN)
    cv = cache_v[0].reshape(n_pool, PAGE_SIZE, D_ATTN)

    def attend_sample(q, k, v):
        new = lambda t: t.reshape(dbs, dseq, D_ATTN)
        return _moba_sample(new(q), new(k), new(v), ck, cv, page_table).reshape(n_s, D_ATTN)

    h1s, h1sb, logit_s, ks, vs, xbc_s, hfin_s = _mixer(
        x_sample.reshape(n_s, D_MODEL), w_in_b, w_out_b, lw, attend_sample, conv0_s, h0_s,
        dbs, dseq, tm=n_s)

    n_p = bsz * seq
    h1 = jnp.concatenate([h1p, h1s], axis=0)
    h1b = jnp.concatenate([h1pb, h1sb], axis=0)
    logits = jnp.concatenate([logit_p, logit_s], axis=0)[:, :N_EXPERTS]
    gates, slot_tok, slot_of_assign, sched = _moe_schedule(logits)
    yb = _moe_experts(h1b[slot_tok], sched, w_gate_up[0], b_gate_up[0], w_down[0], b_down[0])
    y_assign = yb[slot_of_assign].reshape(n_p + n_s, TOP_K * D_MODEL)
    gates_pad = jnp.pad(gates, ((0, 0), (0, LANES - TOP_K)))
    n_all = n_p + n_s
    tm_c = 128
    assert n_all % tm_c == 0
    out = _combine(y_assign, gates_pad, h1, ln2_w[0], ln2_b[0], tm_c)

    y_prompt = out[:n_p].reshape(bsz, seq, D_MODEL)
    y_sample = out[n_p:].reshape(dbs, dseq, D_MODEL)
    heads = lambda t, b, s: t.reshape(1, b, s, N_HEADS, HEAD_DIM)
    k_prompt, v_prompt = heads(kp, bsz, seq), heads(vp, bsz, seq)
    conv_prompt = xbc_p.reshape(bsz, seq, CONV_CH)[:, seq - tail:][None]
    ssm_prompt = hfin_p.reshape(1, bsz, SSM_HEADS, SSM_HEAD_DIM, SSM_STATE)
    k_sample, v_sample = heads(ks, dbs, dseq), heads(vs, dbs, dseq)
    conv_sample = jnp.concatenate([state_conv[0], xbc_s.reshape(dbs, dseq, CONV_CH)],
                                  axis=1)[:, dseq:][None]
    ssm_sample = hfin_s.reshape(1, dbs, SSM_HEADS, SSM_HEAD_DIM, SSM_STATE)
    return (y_prompt, y_sample, k_prompt, v_prompt, conv_prompt, ssm_prompt,
            k_sample, v_sample, conv_sample, ssm_sample)
```

```python
import functools
import math

import jax
import jax.numpy as jnp
from jax import lax
from jax.experimental import pallas as pl
from jax.experimental.pallas import tpu as pltpu

F32 = jnp.float32
BF16 = jnp.bfloat16
HIGHEST = lax.Precision.HIGHEST

D_MODEL = 2048
N_HEADS = 8
HEAD_DIM = 128
D_ATTN = N_HEADS * HEAD_DIM
MOBA_BLOCK = 256
MOBA_TOPK = 3
D_INNER = 1024
SSM_HEAD_DIM = 64
SSM_HEADS = 16
SSM_GROUPS = 2
SSM_STATE = 128
CONV_WIDTH = 4
CONV_CH = D_INNER + 2 * SSM_GROUPS * SSM_STATE
SSD_CHUNK = 128
N_EXPERTS = 32
TOP_K = 4
D_FF = 2048
SWIGLU_LIMIT = 7.0
SWIGLU_ALPHA = 1.702
NORM_EPS = 1e-5
DEEPNORM_ALPHA = 2.0 ** 0.25
PAGE_SIZE = 128

LANES = 128
SUBLANES = 8
VMEM_LIMIT_BYTES = 56 * 1024 * 1024

DT_PAD = LANES
NEG_BIG = -1e30


def _params(*sem):
    return pltpu.CompilerParams(dimension_semantics=sem, vmem_limit_bytes=VMEM_LIMIT_BYTES)


def _nt(a, b, precision=None):
    return lax.dot_general(a, b, (((1,), (1,)), ((), ())), precision=precision,
                           preferred_element_type=F32)


def _mm(a, b, precision=None):
    return jnp.dot(a, b, precision=precision, preferred_element_type=F32)


def _sigmoid(x):
    return 1.0 / (1.0 + jnp.exp(-x))


def _in_proj_kernel(x_ref, w_ref, q_ref, k_ref, v_ref, z_ref, xbc_ref, dt_ref):
    xb = x_ref[...].astype(BF16)
    col = 0
    for ref in (q_ref, k_ref, v_ref, z_ref, xbc_ref, dt_ref):
        width = ref.shape[1]
        ref[...] = _mm(xb, w_ref[:, col:col + width])
        col += width


def _in_proj(x, w_bf16, tm):
    t = x.shape[0]
    widths = (D_ATTN, D_ATTN, D_ATTN, D_INNER, CONV_CH, DT_PAD)
    return pl.pallas_call(
        _in_proj_kernel,
        grid=(t // tm,),
        in_specs=[pl.BlockSpec((tm, D_MODEL), lambda i: (i, 0)),
                  pl.BlockSpec(w_bf16.shape, lambda i: (0, 0), pipeline_mode=pl.Buffered(1))],
        out_specs=[pl.BlockSpec((tm, w), lambda i: (i, 0)) for w in widths],
        out_shape=[jax.ShapeDtypeStruct((t, w), F32) for w in widths],
        compiler_params=_params("parallel"),
        name="in_proj",
    )(x, w_bf16)


def _moba_prompt_kernel(q_ref, k_ref, v_ref, o_ref):
    seq = q_ref.shape[0]
    nb = seq // MOBA_BLOCK
    assert nb <= SUBLANES
    scale = 1.0 / math.sqrt(HEAD_DIM)
    q = q_ref[...]
    k = k_ref[...]
    kmean = jnp.sum(k.reshape(nb, MOBA_BLOCK, HEAD_DIM), axis=1) * (1.0 / MOBA_BLOCK)
    kmean = jnp.concatenate([kmean, jnp.zeros((LANES - nb, HEAD_DIM), F32)], axis=0)
    gate = _nt(kmean, q, HIGHEST)[0:SUBLANES]
    own = lax.broadcasted_iota(jnp.int32, (SUBLANES, seq), 1) // MOBA_BLOCK
    blk = lax.broadcasted_iota(jnp.int32, (SUBLANES, seq), 0)
    rank = jnp.zeros((SUBLANES, seq), jnp.int32)
    for jj in range(nb):
        gj = gate[jj:jj + 1, :]
        ahead = ((gj > gate) | ((gj == gate) & (blk > jj))) & (own > jj)
        rank = rank + ahead.astype(jnp.int32)
    keep = ((blk < own) & (rank < MOBA_TOPK)) | (blk == own)
    pen = jnp.where(keep, 0.0, NEG_BIG)
    pen_t = jnp.concatenate([pen, jnp.zeros((LANES - SUBLANES, seq), F32)], axis=0).T
    lane = lax.broadcasted_iota(jnp.int32, (seq, LANES), 1)
    key_blk = lax.broadcasted_iota(jnp.int32, (seq, LANES), 0) // MOBA_BLOCK
    q_aug = jnp.concatenate([q.astype(BF16), pen_t.astype(BF16)], axis=1)
    k_aug = jnp.concatenate([k.astype(BF16), (lane == key_blk).astype(BF16)], axis=1)
    v_aug = jnp.concatenate([v_ref[...].astype(BF16), (lane == 0).astype(BF16)], axis=1)
    row = lax.broadcasted_iota(jnp.int32, (MOBA_BLOCK, MOBA_BLOCK), 0)
    col = lax.broadcasted_iota(jnp.int32, (MOBA_BLOCK, MOBA_BLOCK), 1)
    causal = col <= row
    for i in range(nb):
        rows = slice(i * MOBA_BLOCK, (i + 1) * MOBA_BLOCK)
        s_blocks = []
        for j in range(i + 1):
            s = _nt(q_aug[rows], k_aug[j * MOBA_BLOCK:(j + 1) * MOBA_BLOCK]) * scale
            if j == i:
                s = jnp.where(causal, s, NEG_BIG)
            s_blocks.append(s)
        mx = s_blocks[0]
        for s in s_blocks[1:]:
            mx = jnp.maximum(mx, s)
        m = mx.max(axis=1, keepdims=True)
        o = jnp.zeros((MOBA_BLOCK, 2 * HEAD_DIM), F32)
        for j, s in enumerate(s_blocks):
            p = jnp.exp(s - m).astype(BF16)
            o = o + _mm(p, v_aug[j * MOBA_BLOCK:(j + 1) * MOBA_BLOCK])
        o_ref[rows, :] = o[:, :HEAD_DIM] / o[:, HEAD_DIM:HEAD_DIM + 1]


def _moba_prompt(q, k, v, bsz, seq):
    spec = pl.BlockSpec((seq, HEAD_DIM), lambda b, h: (b, h))
    return pl.pallas_call(
        _moba_prompt_kernel,
        grid=(bsz, N_HEADS),
        in_specs=[spec, spec, spec],
        out_specs=spec,
        out_shape=jax.ShapeDtypeStruct((bsz * seq, D_ATTN), F32),
        compiler_params=_params("parallel", "parallel"),
        name="moba_prompt",
    )(q, k, v)


SAMPLE_PAGES_PER_STEP = 8


def _moba_sample_kernel(pt_ref, q_ref, kn_ref, vn_ref, *rest, n_new, pps):
    del pt_ref
    k_refs, v_refs = rest[:pps], rest[pps:2 * pps]
    o_ref, st_ref, ksum_ref, acc_ref, l_ref = rest[2 * pps:]
    ph = pl.program_id(1)
    s = pl.program_id(2)
    n_steps = pl.num_programs(2)
    n_pages = st_ref.shape[0]
    ppb = MOBA_BLOCK // PAGE_SIZE
    n_blocks = n_pages // ppb
    scale = 1.0 / math.sqrt(HEAD_DIM)
    head_rows = lambda ref, h: ref[0, pl.ds(h, PAGE_SIZE, stride=N_HEADS), :]

    @pl.when((ph == 0) & (s == 0))
    def _():
        ksum_ref[...] = jnp.zeros_like(ksum_ref)

    @pl.when(ph == 0)
    def _():
        for bi in range(pps // ppb):
            ksum = jnp.zeros((N_HEADS, HEAD_DIM), F32)
            for i in range(bi * ppb, (bi + 1) * ppb):
                pg = s * pps + i
                ksum = ksum + jnp.sum(k_refs[i][0].reshape(PAGE_SIZE, N_HEADS, HEAD_DIM), axis=0)
                for h in range(N_HEADS):
                    kh = head_rows(k_refs[i], h).astype(BF16)
                    st_ref[pg, h] = _nt(q_ref[0, h].astype(BF16), kh) * scale
            blk = s * (pps // ppb) + bi
            for h in range(N_HEADS):
                ksum_ref[h, pl.ds(blk, 1), :] = ksum[h:h + 1, :]

    @pl.when((ph == 1) & (s == 0))
    def _():
        q_idx = lax.broadcasted_iota(jnp.int32, (SUBLANES, 1), 0)
        blk_id = lax.broadcasted_iota(jnp.int32, (SUBLANES, LANES), 1)
        pens = []
        for h in range(N_HEADS):
            gate = _nt(q_ref[0, h], ksum_ref[h] * (1.0 / MOBA_BLOCK), HIGHEST)
            rank = jnp.zeros(gate.shape, jnp.int32)
            for jj in range(n_blocks):
                gj = gate[:, jj:jj + 1]
                ahead = (gj > gate) | ((gj == gate) & (blk_id > jj))
                rank = rank + ahead.astype(jnp.int32)
            sel = (rank < MOBA_TOPK) & (blk_id < n_blocks)
            pens.append(jnp.where(sel, 0.0, NEG_BIG))
        n_keys = n_pages * PAGE_SIZE
        lane_blk = lax.broadcasted_iota(jnp.int32, (LANES, n_keys), 1) // MOBA_BLOCK
        row_blk = lax.broadcasted_iota(jnp.int32, (LANES, n_keys), 0)
        pen = _mm(jnp.concatenate(pens, axis=0).astype(BF16), (lane_blk == row_blk).astype(BF16))
        for h in range(N_HEADS):
            qh = q_ref[0, h]
            pen_h = lambda pg: pen[h * SUBLANES:(h + 1) * SUBLANES,
                                   pg * PAGE_SIZE:(pg + 1) * PAGE_SIZE]
            mx = jnp.full((SUBLANES, PAGE_SIZE), NEG_BIG, F32)
            for pg in range(n_pages):
                mx = jnp.maximum(mx, st_ref[pg, h] + pen_h(pg))
            m = mx.max(axis=1, keepdims=True)
            s_new = []
            for kk in range(n_new):
                sn = jnp.sum(qh * kn_ref[0, h, kk:kk + 1, :], axis=1, keepdims=True) * scale
                sn = jnp.where(q_idx >= kk, sn, NEG_BIG)
                s_new.append(sn)
                m = jnp.maximum(m, sn)
            l = jnp.zeros((SUBLANES, 1), F32)
            acc = jnp.zeros((SUBLANES, HEAD_DIM), F32)
            for kk in range(n_new):
                pn = jnp.exp(s_new[kk] - m)
                l = l + pn
                acc = acc + pn * vn_ref[0, h, kk:kk + 1, :]
            lsum = jnp.zeros((SUBLANES, PAGE_SIZE), F32)
            for pg in range(n_pages):
                pj = jnp.exp(st_ref[pg, h] + pen_h(pg) - m)
                lsum = lsum + pj
                st_ref[pg, h] = pj
            l_ref[h] = l + lsum.sum(axis=1, keepdims=True)
            acc_ref[h] = acc

    @pl.when(ph == 1)
    def _():
        for h in range(N_HEADS):
            acc = acc_ref[h]
            for i in range(pps):
                vh = head_rows(v_refs[i], h).astype(BF16)
                acc = acc + _mm(st_ref[s * pps + i, h].astype(BF16), vh)
            acc_ref[h] = acc

    @pl.when((ph == 1) & (s == n_steps - 1))
    def _():
        for h in range(N_HEADS):
            o_ref[0, h] = acc_ref[h] / l_ref[h]


def _moba_sample(q, k_new, v_new, cache_k, cache_v, page_table):
    bsz, n_new, _ = q.shape
    n_pages = page_table.shape[1]
    pps = SAMPLE_PAGES_PER_STEP
    assert n_new <= SUBLANES and n_pages % pps == 0 and pps % (MOBA_BLOCK // PAGE_SIZE) == 0
    assert n_pages * PAGE_SIZE // MOBA_BLOCK <= LANES
    n_steps = n_pages // pps

    def head_tiles(t):
        t = t.reshape(bsz, n_new, N_HEADS, HEAD_DIM).transpose(0, 2, 1, 3)
        return jnp.pad(t, ((0, 0), (0, 0), (0, SUBLANES - n_new), (0, 0)))

    def k_map(i):
        return lambda b, ph, s, pt: (
            pt[b * n_pages + jnp.where(ph == 0, s * pps + i, n_pages - pps + i)], 0, 0)

    def v_map(i):
        return lambda b, ph, s, pt: (pt[b * n_pages + jnp.where(ph == 1, s * pps + i, i)], 0, 0)

    tile = pl.BlockSpec((1, N_HEADS, SUBLANES, HEAD_DIM), lambda b, ph, s, pt: (b, 0, 0, 0))
    page = lambda index_map: pl.BlockSpec((1, PAGE_SIZE * N_HEADS, HEAD_DIM), index_map)
    grid_spec = pltpu.PrefetchScalarGridSpec(
        num_scalar_prefetch=1,
        grid=(bsz, 2, n_steps),
        in_specs=[tile, tile, tile] + [page(k_map(i)) for i in range(pps)]
        + [page(v_map(i)) for i in range(pps)],
        out_specs=tile,
        scratch_shapes=[pltpu.VMEM((n_pages, N_HEADS, SUBLANES, PAGE_SIZE), F32),
                        pltpu.VMEM((N_HEADS, LANES, HEAD_DIM), F32),
                        pltpu.VMEM((N_HEADS, SUBLANES, HEAD_DIM), F32),
                        pltpu.VMEM((N_HEADS, SUBLANES, 1), F32)])
    o = pl.pallas_call(
        functools.partial(_moba_sample_kernel, n_new=n_new, pps=pps),
        grid_spec=grid_spec,
        out_shape=jax.ShapeDtypeStruct((bsz, N_HEADS, SUBLANES, HEAD_DIM), F32),
        compiler_params=_params("parallel", "arbitrary", "arbitrary"),
        name="moba_sample",
    )(page_table.reshape(-1), head_tiles(q), head_tiles(k_new), head_tiles(v_new),
      *([cache_k] * pps), *([cache_v] * pps))
    return o[:, :, :n_new].transpose(0, 2, 1, 3).reshape(bsz, n_new, D_ATTN)


def _ssd_kernel(xbc_ref, z_ref, dtr_ref, conv0_ref, h0_ref, cw_ref, cb_ref, dtb_ref, alog_ref,
                dskip_ref, nw_ref, y_ref, hfin_ref, tail_ref, xx_ref, st_ref, yd_ref, *, valid_len):
    c = pl.program_id(1)
    nc = pl.num_programs(1)
    q = xbc_ref.shape[0]
    hp = SSM_HEAD_DIM
    gw = D_INNER // SSM_GROUPS
    heads_per_group = SSM_HEADS // SSM_GROUPS

    @pl.when(c == 0)
    def _():
        tail_ref[...] = conv0_ref[0]
        st_ref[...] = h0_ref[0].T

    xbc = xbc_ref[...]
    xx_ref[0:SUBLANES, :] = tail_ref[...]
    xx_ref[SUBLANES:SUBLANES + q, :] = xbc
    tail_ref[...] = xbc[q - SUBLANES:q, :]
    acc = cb_ref[...] + xx_ref[pl.ds(SUBLANES - CONV_WIDTH + 1, q), :] * cw_ref[0:1, :]
    for tap in range(1, CONV_WIDTH):
        acc = acc + xx_ref[pl.ds(SUBLANES - CONV_WIDTH + 1 + tap, q), :] * cw_ref[tap:tap + 1, :]
    xact = acc * _sigmoid(acc)
    xs = xact[:, :D_INNER]
    bm = xact[:, D_INNER:D_INNER + SSM_GROUPS * SSM_STATE]
    cm = xact[:, D_INNER + SSM_GROUPS * SSM_STATE:]

    v = dtr_ref[...] + dtb_ref[...]
    dt = jnp.maximum(v, 0.0) + jnp.log1p(jnp.exp(-jnp.abs(v)))
    if valid_len is not None:
        t_idx = c * q + lax.broadcasted_iota(jnp.int32, dt.shape, 0)
        dt = jnp.where(t_idx < valid_len, dt, 0.0)
    a = -jnp.exp(alog_ref[...])
    r_i = lax.broadcasted_iota(jnp.int32, (q, q), 0)
    c_i = lax.broadcasted_iota(jnp.int32, (q, q), 1)
    tril = c_i <= r_i
    acs = _mm(tril.astype(F32), dt * a, HIGHEST)
    acs_last = acs[q - 1:q, :]
    acs_t = acs.T
    e_r = lax.broadcasted_iota(jnp.int32, (LANES, D_INNER), 0)
    e_c = lax.broadcasted_iota(jnp.int32, (LANES, D_INNER), 1)
    expand = (e_c // hp == e_r).astype(F32)
    dt_e = _mm(dt, expand, HIGHEST)
    to_end_e = _mm(jnp.exp(acs_last - acs) * dt, expand, HIGHEST)
    eacs_e = _mm(jnp.exp(acs), expand, HIGHEST)
    cdec_e = _mm(jnp.exp(acs_last), expand, HIGHEST)

    xdt_b = (xs * dt_e).astype(BF16)
    xw_b = (xs * to_end_e).astype(BF16)
    for g in range(SSM_GROUPS):
        bg = bm[:, g * SSM_STATE:(g + 1) * SSM_STATE]
        cg_b = cm[:, g * SSM_STATE:(g + 1) * SSM_STATE].astype(BF16)
        cb = _nt(cg_b, bg.astype(BF16))
        for hh in range(heads_per_group):
            h = g * heads_per_group + hh
            seg = acs[:, h:h + 1] - acs_t[h:h + 1, :]
            decay = jnp.where(tril, jnp.exp(jnp.minimum(seg, 0.0)), 0.0)
            yd_ref[:, h * hp:(h + 1) * hp] = _mm((cb * decay).astype(BF16),
                                                 xdt_b[:, h * hp:(h + 1) * hp])
        gc = slice(g * gw, (g + 1) * gw)
        st_g = st_ref[:, gc]
        y_off = _mm(cg_b, st_g.astype(BF16)) * eacs_e[:, gc]
        yd_ref[:, gc] += y_off
        st_ref[:, gc] = st_g * cdec_e[:, gc] + _mm(bg.T.astype(BF16), xw_b[:, gc])

    y = yd_ref[...] + dskip_ref[...] * xs
    zz = z_ref[...]
    gated = y * (zz * _sigmoid(zz))
    for g in range(SSM_GROUPS):
        gc = slice(g * gw, (g + 1) * gw)
        gg = gated[:, gc]
        ms = jnp.mean(gg * gg, axis=1, keepdims=True)
        y_ref[:, gc] = gg * lax.rsqrt(ms + NORM_EPS) * nw_ref[:, gc]

    @pl.when(c == nc - 1)
    def _():
        hfin_ref[0] = st_ref[...].T


def _ssd(xbc, z, dtr, conv0, h0, conv_w, conv_b, dt_bias, a_log, d_skip, norm_w, bsz, seq,
         valid_len=None):
    q = min(SSD_CHUNK, seq)
    nc = seq // q
    pad16 = lambda t: jnp.pad(t.reshape(1, SSM_HEADS), ((0, 0), (0, DT_PAD - SSM_HEADS)))
    row = lambda w: pl.BlockSpec((q, w), lambda b, c: (b * nc + c, 0))
    const = lambda shape: pl.BlockSpec(shape, lambda b, c: (0,) * len(shape))
    per_b = lambda shape: pl.BlockSpec((1,) + shape, lambda b, c: (b, 0, 0))
    return pl.pallas_call(
        functools.partial(_ssd_kernel, valid_len=valid_len),
        grid=(bsz, nc),
        in_specs=[row(CONV_CH), row(D_INNER), row(DT_PAD),
                  per_b((SUBLANES, CONV_CH)), per_b((D_INNER, SSM_STATE)),
                  const((CONV_WIDTH, CONV_CH)), const((1, CONV_CH)),
                  const((1, DT_PAD)), const((1, DT_PAD)), const((1, D_INNER)),
                  const((1, D_INNER))],
        out_specs=[row(D_INNER), per_b((D_INNER, SSM_STATE))],
        out_shape=[jax.ShapeDtypeStruct((bsz * seq, D_INNER), F32),
                   jax.ShapeDtypeStruct((bsz, D_INNER, SSM_STATE), F32)],
        scratch_shapes=[pltpu.VMEM((SUBLANES, CONV_CH), F32),
                        pltpu.VMEM((q + SUBLANES, CONV_CH), F32),
                        pltpu.VMEM((SSM_STATE, D_INNER), F32),
                        pltpu.VMEM((q, D_INNER), F32)],
        compiler_params=_params("parallel", "arbitrary"),
        name="ssd",
    )(xbc, z, dtr, conv0, h0, conv_w, conv_b.reshape(1, CONV_CH), pad16(dt_bias), pad16(a_log),
      jnp.repeat(d_skip, SSM_HEAD_DIM).reshape(1, D_INNER), norm_w.reshape(1, D_INNER))


def _layer_norm(x, w, b):
    mu = jnp.mean(x, axis=-1, keepdims=True)
    xc = x - mu
    var = jnp.mean(xc * xc, axis=-1, keepdims=True)
    return xc * lax.rsqrt(var + NORM_EPS) * w + b


def _out_proj_kernel(attn_ref, ssd_ref, x_ref, w_ref, lnw_ref, lnb_ref, wr2_ref, wr1_ref, br_ref,
                     *rest):
    h_ref, hb_ref, logit_ref = rest[-3:]
    mixed = (_mm(attn_ref[...].astype(BF16), w_ref[0:D_ATTN, :])
             + _mm(ssd_ref[...].astype(BF16), w_ref[D_ATTN:, :]))
    h = _layer_norm(DEEPNORM_ALPHA * x_ref[...] + mixed, lnw_ref[...], lnb_ref[...])
    h_ref[...] = h
    h_hi = h.astype(BF16)
    hb_ref[...] = h_hi
    h_lo = (h - h_hi.astype(F32)).astype(BF16)
    part = _mm(h_hi, wr2_ref[...])
    logit_ref[...] = part[:, :LANES] + part[:, LANES:] + _mm(h_lo, wr1_ref[...]) + br_ref[...]


def _out_proj(attn, ssd, x, w_bf16, lw, tm, n_rows, row_offset=0, into=None):
    t = x.shape[0]
    assert row_offset % tm == 0 and t % tm == 0
    off = row_offset // tm
    row = lambda w: pl.BlockSpec((tm, w), lambda i: (i, 0))
    out_row = lambda w: pl.BlockSpec((tm, w), lambda i: (off + i, 0))
    const = lambda shape: pl.BlockSpec(shape, lambda i: (0, 0))
    in_specs = [row(D_ATTN), row(D_INNER), row(D_MODEL), const((D_MODEL, D_MODEL)),
                const((1, D_MODEL)), const((1, D_MODEL)), const((D_MODEL, 2 * LANES)),
                const((D_MODEL, LANES)), const((1, LANES))]
    args = [attn, ssd, x, w_bf16, lw["ln1_w"].reshape(1, D_MODEL), lw["ln1_b"].reshape(1, D_MODEL),
            lw["w_router_hl"], lw["w_router_hi"], lw["b_router_pad"]]
    aliases = {}
    if into is not None:
        aliases = {len(args) + j: j for j in range(len(into))}
        in_specs += [pl.BlockSpec(memory_space=pl.ANY)] * len(into)
        args += list(into)
    return pl.pallas_call(
        _out_proj_kernel,
        grid=(t // tm,),
        in_specs=in_specs,
        out_specs=[out_row(D_MODEL), out_row(D_MODEL), out_row(LANES)],
        out_shape=[jax.ShapeDtypeStruct((n_rows, D_MODEL), F32),
                   jax.ShapeDtypeStruct((n_rows, D_MODEL), BF16),
                   jax.ShapeDtypeStruct((n_rows, LANES), F32)],
        input_output_aliases=aliases,
        compiler_params=_params("parallel"),
        name="out_proj",
    )(*args)


MOE_BM = 256
MOE_TF = 1024


def _gate_up_kernel(e_ref, n_ref, r_ref, first_ref, valid_ref, x_ref, wg_ref, wu_ref, bg_ref,
                    bu_ref, act_ref, wgb_ref, wub_ref):
    i = pl.program_id(0)

    @pl.when(first_ref[i] == 1)
    def _():
        wgb_ref[...] = wg_ref[0].astype(BF16)
        wub_ref[...] = wu_ref[0].astype(BF16)

    @pl.when(valid_ref[i] == 1)
    def _():
        x = x_ref[...]
        g = jnp.minimum(_mm(x, wgb_ref[...]) + bg_ref[0], SWIGLU_LIMIT)
        u = jnp.clip(_mm(x, wub_ref[...]) + bu_ref[0], -SWIGLU_LIMIT, SWIGLU_LIMIT)
        act_ref[...] = ((u + 1.0) * (g * _sigmoid(SWIGLU_ALPHA * g))).astype(BF16)

    @pl.when(valid_ref[i] == 0)
    def _():
        act_ref[...] = jnp.zeros_like(act_ref)


def _down_kernel(e_ref, r_ref, first_ref, valid_ref, a_ref, wd_ref, bd_ref, y_ref, wdb_ref):
    i = pl.program_id(0)

    @pl.when(first_ref[i] == 1)
    def _():
        wdb_ref[...] = wd_ref[0].astype(BF16)

    @pl.when(valid_ref[i] == 1)
    def _():
        y_ref[...] = _mm(a_ref[...], wdb_ref[...]) + bd_ref[0]

    @pl.when(valid_ref[i] == 0)
    def _():
        y_ref[...] = jnp.zeros_like(y_ref)


def _moe_experts(xs, sched, w_gate_up, b_gate_up, w_down, b_down):
    cap = xs.shape[0]
    n_blocks = cap // MOE_BM
    nt = D_FF // MOE_TF
    up_off = D_FF // MOE_TF
    b_gu = b_gate_up.reshape(N_EXPERTS, 1, 2 * D_FF)
    gu_spec = pltpu.PrefetchScalarGridSpec(
        num_scalar_prefetch=5,
        grid=(n_blocks * nt,),
        in_specs=[
            pl.BlockSpec((MOE_BM, D_MODEL), lambda i, e, n, r, f, v: (r[i], 0)),
            pl.BlockSpec((1, D_MODEL, MOE_TF), lambda i, e, n, r, f, v: (e[i], 0, n[i])),
            pl.BlockSpec((1, D_MODEL, MOE_TF), lambda i, e, n, r, f, v: (e[i], 0, up_off + n[i])),
            pl.BlockSpec((1, 1, MOE_TF), lambda i, e, n, r, f, v: (e[i], 0, n[i])),
            pl.BlockSpec((1, 1, MOE_TF), lambda i, e, n, r, f, v: (e[i], 0, up_off + n[i])),
        ],
        out_specs=pl.BlockSpec((MOE_BM, MOE_TF), lambda i, e, n, r, f, v: (r[i], n[i])),
        scratch_shapes=[pltpu.VMEM((D_MODEL, MOE_TF), BF16), pltpu.VMEM((D_MODEL, MOE_TF), BF16)])
    act = pl.pallas_call(
        _gate_up_kernel,
        grid_spec=gu_spec,
        out_shape=jax.ShapeDtypeStruct((cap, D_FF), BF16),
        compiler_params=_params("arbitrary"),
        name="moe_gate_up",
    )(sched["gu_e"], sched["gu_n"], sched["gu_r"], sched["gu_first"], sched["gu_valid"],
      xs, w_gate_up, w_gate_up, b_gu, b_gu)

    down_spec = pltpu.PrefetchScalarGridSpec(
        num_scalar_prefetch=4,
        grid=(n_blocks,),
        in_specs=[
            pl.BlockSpec((MOE_BM, D_FF), lambda i, e, r, f, v: (r[i], 0)),
            pl.BlockSpec((1, D_FF, D_MODEL), lambda i, e, r, f, v: (e[i], 0, 0)),
            pl.BlockSpec((1, 1, D_MODEL), lambda i, e, r, f, v: (e[i], 0, 0)),
        ],
        out_specs=pl.BlockSpec((MOE_BM, D_MODEL), lambda i, e, r, f, v: (r[i], 0)),
        scratch_shapes=[pltpu.VMEM((D_FF, D_MODEL), BF16)])
    return pl.pallas_call(
        _down_kernel,
        grid_spec=down_spec,
        out_shape=jax.ShapeDtypeStruct((cap, D_MODEL), F32),
        compiler_params=_params("arbitrary"),
        name="moe_down",
    )(sched["d_e"], sched["d_r"], sched["d_first"], sched["d_valid"],
      act, w_down, b_down.reshape(N_EXPERTS, 1, D_MODEL))


def _route_kernel(logit_ref, idx_ref, gate_ref, pos_ref, cnt_ref, carry_ref):
    i = pl.program_id(0)
    tm = logit_ref.shape[0]

    @pl.when(i == 0)
    def _():
        carry_ref[...] = jnp.zeros_like(carry_ref)

    lane = lax.broadcasted_iota(jnp.int32, (tm, LANES), 1).astype(F32)
    lg = jnp.where(lane < N_EXPERTS, logit_ref[...], -jnp.inf)
    sel = jnp.zeros((tm, LANES), F32)
    vals, idxs = [], []
    for _ in range(TOP_K):
        mx = lg.max(axis=1, keepdims=True)
        ik = jnp.min(jnp.where(lg == mx, lane, float(LANES)), axis=1, keepdims=True)
        hit = lane == ik
        sel = sel + hit.astype(F32)
        lg = jnp.where(hit, -jnp.inf, lg)
        vals.append(mx)
        idxs.append(ik)
    ex = [jnp.exp(v - vals[0]) for v in vals]
    den = ex[0]
    for e in ex[1:]:
        den = den + e
    r_i = lax.broadcasted_iota(jnp.int32, (tm, tm), 0)
    c_i = lax.broadcasted_iota(jnp.int32, (tm, tm), 1)
    before = _mm((c_i < r_i).astype(BF16), sel.astype(BF16)) + carry_ref[...]
    idx_out = jnp.zeros((tm, LANES), F32)
    gate_out = jnp.zeros((tm, LANES), F32)
    pos_out = jnp.zeros((tm, LANES), F32)
    for kk in range(TOP_K):
        pos = jnp.sum(jnp.where(lane == idxs[kk], before, 0.0), axis=1, keepdims=True)
        idx_out = jnp.where(lane == kk, idxs[kk], idx_out)
        gate_out = jnp.where(lane == kk, ex[kk] / den, gate_out)
        pos_out = jnp.where(lane == kk, pos, pos_out)
    idx_ref[...] = idx_out.astype(jnp.int32)
    gate_ref[...] = gate_out
    pos_ref[...] = pos_out.astype(jnp.int32)
    carry_ref[...] += jnp.sum(sel, axis=0, keepdims=True)
    cnt_ref[...] = carry_ref[...].astype(jnp.int32)


def _route(logits, tm):
    t = logits.shape[0]
    row = pl.BlockSpec((tm, LANES), lambda i: (i, 0))
    return pl.pallas_call(
        _route_kernel,
        grid=(t // tm,),
        in_specs=[row],
        out_specs=[row, row, row, pl.BlockSpec((1, LANES), lambda i: (0, 0))],
        out_shape=[jax.ShapeDtypeStruct((t, LANES), jnp.int32),
                   jax.ShapeDtypeStruct((t, LANES), F32),
                   jax.ShapeDtypeStruct((t, LANES), jnp.int32),
                   jax.ShapeDtypeStruct((1, LANES), jnp.int32)],
        scratch_shapes=[pltpu.VMEM((1, LANES), F32)],
        compiler_params=_params("arbitrary"),
        name="moe_route",
    )(logits)


def _moe_schedule(idx, pos, counts):
    i32 = jnp.int32
    n_tok = idx.shape[0]
    n_assign = n_tok * TOP_K
    blocks_e = (counts + MOE_BM - 1) // MOE_BM
    blk_end = jnp.cumsum(blocks_e).astype(i32)
    blk_start = blk_end - blocks_e
    grp_start = (jnp.cumsum(counts) - counts).astype(i32)
    slot_of_assign = blk_start[idx] * MOE_BM + pos
    n_blocks = -(-n_assign // MOE_BM) + N_EXPERTS
    cap = n_blocks * MOE_BM
    blk = jnp.arange(n_blocks, dtype=i32)
    expert_of = lambda ends, x: jnp.minimum(
        jnp.sum((x[:, None] >= ends[None, :]).astype(i32), axis=1), N_EXPERTS - 1)
    blk_e = expert_of(blk_end, blk)
    blk_valid = (blk < blk_end[-1]).astype(i32)
    key = idx.reshape(-1) * n_assign + jnp.arange(n_assign, dtype=i32)
    sorted_assign = jnp.sort(key) % n_assign
    slot = jnp.arange(cap, dtype=i32)
    slot_e = blk_e[slot // MOE_BM]
    local = slot - blk_start[slot_e] * MOE_BM
    src = jnp.clip(grp_start[slot_e] + local, 0, n_assign - 1)
    slot_tok = jnp.where(local < counts[slot_e], sorted_assign[src] // TOP_K, 0)
    nblk_e = blocks_e.at[N_EXPERTS - 1].add(n_blocks - blk_end[-1])
    nt = D_FF // MOE_TF
    item = jnp.arange(n_blocks * nt, dtype=i32)
    it_e = expert_of(jnp.cumsum(nblk_e * nt).astype(i32), item)
    local_item = item - blk_start[it_e] * nt
    per = jnp.maximum(nblk_e[it_e], 1)
    it_n = local_item // per
    it_lr = local_item % per
    it_r = blk_start[it_e] + it_lr
    sched = {"gu_e": it_e, "gu_n": it_n, "gu_r": it_r, "gu_first": (it_lr == 0).astype(i32),
             "gu_valid": blk_valid[it_r],
             "d_e": blk_e, "d_r": blk, "d_first": (blk == blk_start[blk_e]).astype(i32),
             "d_valid": blk_valid}
    return slot_tok, slot_of_assign, sched


def _combine_kernel(y_ref, g_ref, h_ref, lnw_ref, lnb_ref, op_ref, os_ref, *, n_first):
    i = pl.program_id(0)
    g = g_ref[...]
    ff = y_ref[0] * g[:, 0:1]
    for kk in range(1, TOP_K):
        ff = ff + y_ref[kk] * g[:, kk:kk + 1]
    out = _layer_norm(DEEPNORM_ALPHA * h_ref[...] + ff, lnw_ref[...], lnb_ref[...])

    @pl.when(i < n_first)
    def _():
        op_ref[...] = out

    @pl.when(i >= n_first)
    def _():
        os_ref[...] = out


def _combine(y_assign, gates, h, ln_w, ln_b, n_p, tm):
    t = h.shape[0]
    assert n_p % tm == 0 and t % tm == 0
    n_first = n_p // tm
    row = lambda w: pl.BlockSpec((tm, w), lambda i: (i, 0))
    const = lambda shape: pl.BlockSpec(shape, lambda i: (0, 0))
    return pl.pallas_call(
        functools.partial(_combine_kernel, n_first=n_first),
        grid=(t // tm,),
        in_specs=[pl.BlockSpec((TOP_K, tm, D_MODEL), lambda i: (0, i, 0)), row(LANES),
                  row(D_MODEL), const((1, D_MODEL)), const((1, D_MODEL))],
        out_specs=[pl.BlockSpec((tm, D_MODEL), lambda i: (jnp.minimum(i, n_first - 1), 0)),
                   pl.BlockSpec((tm, D_MODEL), lambda i: (jnp.maximum(i - n_first, 0), 0))],
        out_shape=[jax.ShapeDtypeStruct((n_p, D_MODEL), F32),
                   jax.ShapeDtypeStruct((t - n_p, D_MODEL), F32)],
        compiler_params=_params("arbitrary"),
        name="moe_combine",
    )(y_assign, gates, h, ln_w.reshape(1, D_MODEL), ln_b.reshape(1, D_MODEL))


def _mixer(x, w_in_b, w_out_b, lw, attend, conv0, h0, bsz, seq, tm, n_rows, row_offset=0,
           into=None):
    q, k, v, z, xbc, dtr = _in_proj(x, w_in_b, tm)
    attn = attend(q, k, v)
    if seq % SSD_CHUNK == 0:
        ssd, h_fin = _ssd(xbc, z, dtr, conv0, h0, lw["conv_w"], lw["conv_b"], lw["dt_bias"],
                          lw["a_log"], lw["d_skip"], lw["ssm_norm_w"], bsz, seq)
    else:
        assert seq < SSD_CHUNK
        pad = lambda t: jnp.pad(t.reshape(bsz, seq, -1),
                                ((0, 0), (0, SSD_CHUNK - seq), (0, 0))).reshape(bsz * SSD_CHUNK, -1)
        ssd, h_fin = _ssd(pad(xbc), pad(z), pad(dtr), conv0, h0, lw["conv_w"], lw["conv_b"],
                          lw["dt_bias"], lw["a_log"], lw["d_skip"], lw["ssm_norm_w"], bsz,
                          SSD_CHUNK, valid_len=seq)
        ssd = ssd.reshape(bsz, SSD_CHUNK, D_INNER)[:, :seq].reshape(bsz * seq, D_INNER)
    bufs = _out_proj(attn, ssd, x, w_out_b, lw, tm, n_rows, row_offset, into)
    return bufs, k, v, xbc, h_fin


def kernel(x_prompt, x_sample, cache_k, cache_v, state_conv, state_ssm, page_table, w_in, conv_w,
           conv_b, dt_bias, a_log, d_skip, ssm_norm_w, w_out, ln1_w, ln1_b, w_router, b_router,
           w_gate_up, b_gate_up, w_down, b_down, ln2_w, ln2_b):
    depth = w_in.shape[0]
    assert depth == 1
    bsz, seq, _ = x_prompt.shape
    dbs, dseq, _ = x_sample.shape
    n_pool = cache_k.shape[1]
    tail = CONV_WIDTH - 1
    n_p, n_s = bsz * seq, dbs * dseq
    n_all = n_p + n_s
    wr = jnp.pad(w_router[0], ((0, 0), (0, LANES - N_EXPERTS)))
    wr_hi = wr.astype(BF16)
    wr_lo = (wr - wr_hi.astype(F32)).astype(BF16)
    lw = {"conv_w": conv_w[0], "conv_b": conv_b[0], "dt_bias": dt_bias[0], "a_log": a_log[0],
          "d_skip": d_skip[0], "ssm_norm_w": ssm_norm_w[0], "ln1_w": ln1_w[0], "ln1_b": ln1_b[0],
          "w_router_hl": jnp.concatenate([wr_hi, wr_lo], axis=1), "w_router_hi": wr_hi,
          "b_router_pad": jnp.pad(b_router[0], (0, LANES - N_EXPERTS)).reshape(1, LANES)}
    w_in_b = jnp.pad(w_in[0], ((0, 0), (0, DT_PAD - SSM_HEADS))).astype(BF16)
    w_out_b = w_out[0].astype(BF16)

    conv0_p = jnp.zeros((bsz, SUBLANES, CONV_CH), F32)
    h0_p = jnp.zeros((bsz, D_INNER, SSM_STATE), F32)
    bufs, kp, vp, xbc_p, hfin_p = _mixer(
        x_prompt.reshape(n_p, D_MODEL), w_in_b, w_out_b, lw,
        lambda q, k, v: _moba_prompt(q, k, v, bsz, seq), conv0_p, h0_p, bsz, seq, tm=256,
        n_rows=n_all)

    conv0_s = jnp.pad(state_conv[0], ((0, 0), (SUBLANES - tail, 0), (0, 0)))
    h0_s = state_ssm[0].reshape(dbs, D_INNER, SSM_STATE)
    ck = cache_k[0].reshape(n_pool, PAGE_SIZE * N_HEADS, HEAD_DIM)
    cv = cache_v[0].reshape(n_pool, PAGE_SIZE * N_HEADS, HEAD_DIM)

    def attend_sample(q, k, v):
        new = lambda t: t.reshape(dbs, dseq, D_ATTN)
        return _moba_sample(new(q), new(k), new(v), ck, cv, page_table).reshape(n_s, D_ATTN)

    (h1, h1b, logits), ks, vs, xbc_s, hfin_s = _mixer(
        x_sample.reshape(n_s, D_MODEL), w_in_b, w_out_b, lw, attend_sample, conv0_s, h0_s,
        dbs, dseq, tm=n_s, n_rows=n_all, row_offset=n_p, into=bufs)

    tm_c = 128
    idx, gates, pos, counts = _route(logits, tm_c)
    slot_tok, slot_of_assign, sched = _moe_schedule(idx[:, :TOP_K], pos[:, :TOP_K],
                                                    counts[0, :N_EXPERTS])
    yb = _moe_experts(h1b[slot_tok], sched, w_gate_up[0], b_gate_up[0], w_down[0], b_down[0])
    y_assign = yb[slot_of_assign.T.reshape(-1)].reshape(TOP_K, n_all, D_MODEL)
    out_p, out_s = _combine(y_assign, gates, h1, ln2_w[0], ln2_b[0], n_p, tm_c)

    y_prompt = out_p.reshape(bsz, seq, D_MODEL)
    y_sample = out_s.reshape(dbs, dseq, D_MODEL)
    heads = lambda t, b, s: t.reshape(1, b, s, N_HEADS, HEAD_DIM)
    k_prompt, v_prompt = heads(kp, bsz, seq), heads(vp, bsz, seq)
    conv_prompt = xbc_p.reshape(bsz, seq, CONV_CH)[:, seq - tail:][None]
    ssm_prompt = hfin_p.reshape(1, bsz, SSM_HEADS, SSM_HEAD_DIM, SSM_STATE)
    k_sample, v_sample = heads(ks, dbs, dseq), heads(vs, dbs, dseq)
    conv_sample = jnp.concatenate([state_conv[0], xbc_s.reshape(dbs, dseq, CONV_CH)],
                                  axis=1)[:, dseq:][None]
    ssm_sample = hfin_s.reshape(1, dbs, SSM_HEADS, SSM_HEAD_DIM, SSM_STATE)
    return (y_prompt, y_sample, k_prompt, v_prompt, conv_prompt, ssm_prompt,
            k_sample, v_sample, conv_sample, ssm_sample)
```

```python
import functools
import math

import jax
import jax.numpy as jnp
from jax import lax
from jax.experimental import pallas as pl
from jax.experimental.pallas import tpu as pltpu

F32 = jnp.float32
BF16 = jnp.bfloat16
HIGHEST = lax.Precision.HIGHEST

D_MODEL = 2048
N_HEADS = 8
HEAD_DIM = 128
D_ATTN = N_HEADS * HEAD_DIM
MOBA_BLOCK = 256
MOBA_TOPK = 3
D_INNER = 1024
SSM_HEAD_DIM = 64
SSM_HEADS = 16
SSM_GROUPS = 2
SSM_STATE = 128
CONV_WIDTH = 4
CONV_CH = D_INNER + 2 * SSM_GROUPS * SSM_STATE
SSD_CHUNK = 128
N_EXPERTS = 32
TOP_K = 4
D_FF = 2048
SWIGLU_LIMIT = 7.0
SWIGLU_ALPHA = 1.702
NORM_EPS = 1e-5
DEEPNORM_ALPHA = 2.0 ** 0.25
PAGE_SIZE = 128

LANES = 128
SUBLANES = 8
VMEM_LIMIT_BYTES = 56 * 1024 * 1024

DT_PAD = LANES
NEG_BIG = -1e30


def _params(*sem):
    return pltpu.CompilerParams(dimension_semantics=sem, vmem_limit_bytes=VMEM_LIMIT_BYTES)


def _nt(a, b, precision=None):
    return lax.dot_general(a, b, (((1,), (1,)), ((), ())), precision=precision,
                           preferred_element_type=F32)


def _mm(a, b, precision=None):
    return jnp.dot(a, b, precision=precision, preferred_element_type=F32)


def _mm_01(m01, x, pieces, left=False):
    out = None
    rest = x
    for _ in range(pieces):
        part = rest.astype(BF16)
        rest = rest - part.astype(F32)
        term = _mm(m01, part) if left else _mm(part, m01)
        out = term if out is None else out + term
    return out


def _sigmoid(x):
    return 1.0 / (1.0 + jnp.exp(-x))


def _in_proj_kernel(x_ref, w_ref, q_ref, k_ref, v_ref, z_ref, xbc_ref, dt_ref):
    xb = x_ref[...].astype(BF16)
    col = 0
    for ref in (q_ref, k_ref, v_ref, z_ref, xbc_ref, dt_ref):
        width = ref.shape[1]
        ref[...] = _mm(xb, w_ref[:, col:col + width])
        col += width


def _in_proj(x, w_bf16, tm):
    t = x.shape[0]
    widths = (D_ATTN, D_ATTN, D_ATTN, D_INNER, CONV_CH, DT_PAD)
    return pl.pallas_call(
        _in_proj_kernel,
        grid=(t // tm,),
        in_specs=[pl.BlockSpec((tm, D_MODEL), lambda i: (i, 0)),
                  pl.BlockSpec(w_bf16.shape, lambda i: (0, 0), pipeline_mode=pl.Buffered(1))],
        out_specs=[pl.BlockSpec((tm, w), lambda i: (i, 0)) for w in widths],
        out_shape=[jax.ShapeDtypeStruct((t, w), F32) for w in widths],
        compiler_params=_params("parallel"),
        name="in_proj",
    )(x, w_bf16)


def _moba_prompt_kernel(q_ref, k_ref, v_ref, o_ref):
    seq = q_ref.shape[0]
    nb = seq // MOBA_BLOCK
    assert nb <= SUBLANES
    scale = 1.0 / math.sqrt(HEAD_DIM)
    q = q_ref[...]
    k = k_ref[...]
    kmean = jnp.sum(k.reshape(nb, MOBA_BLOCK, HEAD_DIM), axis=1) * (1.0 / MOBA_BLOCK)
    kmean = jnp.concatenate([kmean, jnp.zeros((LANES - nb, HEAD_DIM), F32)], axis=0)
    gate = _nt(kmean, q, HIGHEST)[0:SUBLANES]
    own = lax.broadcasted_iota(jnp.int32, (SUBLANES, seq), 1) // MOBA_BLOCK
    blk = lax.broadcasted_iota(jnp.int32, (SUBLANES, seq), 0)
    rank = jnp.zeros((SUBLANES, seq), jnp.int32)
    for jj in range(nb):
        gj = gate[jj:jj + 1, :]
        ahead = ((gj > gate) | ((gj == gate) & (blk > jj))) & (own > jj)
        rank = rank + ahead.astype(jnp.int32)
    keep = ((blk < own) & (rank < MOBA_TOPK)) | (blk == own)
    pen = jnp.where(keep, 0.0, NEG_BIG)
    pen_t = jnp.concatenate([pen, jnp.zeros((LANES - SUBLANES, seq), F32)], axis=0).T
    lane = lax.broadcasted_iota(jnp.int32, (seq, LANES), 1)
    key_blk = lax.broadcasted_iota(jnp.int32, (seq, LANES), 0) // MOBA_BLOCK
    q_aug = jnp.concatenate([q.astype(BF16), pen_t.astype(BF16)], axis=1)
    k_aug = jnp.concatenate([k.astype(BF16), (lane == key_blk).astype(BF16)], axis=1)
    v_aug = jnp.concatenate([v_ref[...].astype(BF16), (lane == 0).astype(BF16)], axis=1)
    row = lax.broadcasted_iota(jnp.int32, (MOBA_BLOCK, MOBA_BLOCK), 0)
    col = lax.broadcasted_iota(jnp.int32, (MOBA_BLOCK, MOBA_BLOCK), 1)
    causal = col <= row
    for i in range(nb):
        rows = slice(i * MOBA_BLOCK, (i + 1) * MOBA_BLOCK)
        s_blocks = []
        for j in range(i + 1):
            s = _nt(q_aug[rows], k_aug[j * MOBA_BLOCK:(j + 1) * MOBA_BLOCK]) * scale
            if j == i:
                s = jnp.where(causal, s, NEG_BIG)
            s_blocks.append(s)
        mx = s_blocks[0]
        for s in s_blocks[1:]:
            mx = jnp.maximum(mx, s)
        m = mx.max(axis=1, keepdims=True)
        o = jnp.zeros((MOBA_BLOCK, 2 * HEAD_DIM), F32)
        for j, s in enumerate(s_blocks):
            p = jnp.exp(s - m).astype(BF16)
            o = o + _mm(p, v_aug[j * MOBA_BLOCK:(j + 1) * MOBA_BLOCK])
        o_ref[rows, :] = o[:, :HEAD_DIM] / o[:, HEAD_DIM:HEAD_DIM + 1]


def _moba_prompt(q, k, v, bsz, seq):
    spec = pl.BlockSpec((seq, HEAD_DIM), lambda b, h: (b, h))
    return pl.pallas_call(
        _moba_prompt_kernel,
        grid=(bsz, N_HEADS),
        in_specs=[spec, spec, spec],
        out_specs=spec,
        out_shape=jax.ShapeDtypeStruct((bsz * seq, D_ATTN), F32),
        compiler_params=_params("parallel", "parallel"),
        name="moba_prompt",
    )(q, k, v)


SAMPLE_PAGES_PER_STEP = 8


def _moba_sample_kernel(pt_ref, q_ref, kn_ref, vn_ref, *rest, n_new, pps):
    del pt_ref
    k_refs, v_refs = rest[:pps], rest[pps:2 * pps]
    o_ref, st_ref, ksum_ref, acc_ref, l_ref = rest[2 * pps:]
    ph = pl.program_id(1)
    s = pl.program_id(2)
    n_steps = pl.num_programs(2)
    n_pages = st_ref.shape[0]
    ppb = MOBA_BLOCK // PAGE_SIZE
    n_blocks = n_pages // ppb
    scale = 1.0 / math.sqrt(HEAD_DIM)
    head_rows = lambda ref, h: ref[0, pl.ds(h, PAGE_SIZE, stride=N_HEADS), :]

    @pl.when((ph == 0) & (s == 0))
    def _():
        ksum_ref[...] = jnp.zeros_like(ksum_ref)

    @pl.when(ph == 0)
    def _():
        for bi in range(pps // ppb):
            ksum = jnp.zeros((N_HEADS, HEAD_DIM), F32)
            for i in range(bi * ppb, (bi + 1) * ppb):
                pg = s * pps + i
                ksum = ksum + jnp.sum(k_refs[i][0].reshape(PAGE_SIZE, N_HEADS, HEAD_DIM), axis=0)
                for h in range(N_HEADS):
                    kh = head_rows(k_refs[i], h).astype(BF16)
                    st_ref[pg, h] = _nt(q_ref[0, h].astype(BF16), kh) * scale
            blk = s * (pps // ppb) + bi
            for h in range(N_HEADS):
                ksum_ref[h, pl.ds(blk, 1), :] = ksum[h:h + 1, :]

    @pl.when((ph == 1) & (s == 0))
    def _():
        q_idx = lax.broadcasted_iota(jnp.int32, (SUBLANES, 1), 0)
        blk_id = lax.broadcasted_iota(jnp.int32, (SUBLANES, LANES), 1)
        pens = []
        for h in range(N_HEADS):
            gate = _nt(q_ref[0, h], ksum_ref[h] * (1.0 / MOBA_BLOCK), HIGHEST)
            rank = jnp.zeros(gate.shape, jnp.int32)
            for jj in range(n_blocks):
                gj = gate[:, jj:jj + 1]
                ahead = (gj > gate) | ((gj == gate) & (blk_id > jj))
                rank = rank + ahead.astype(jnp.int32)
            sel = (rank < MOBA_TOPK) & (blk_id < n_blocks)
            pens.append(jnp.where(sel, 0.0, NEG_BIG))
        n_keys = n_pages * PAGE_SIZE
        lane_blk = lax.broadcasted_iota(jnp.int32, (LANES, n_keys), 1) // MOBA_BLOCK
        row_blk = lax.broadcasted_iota(jnp.int32, (LANES, n_keys), 0)
        pen = _mm(jnp.concatenate(pens, axis=0).astype(BF16), (lane_blk == row_blk).astype(BF16))
        for h in range(N_HEADS):
            qh = q_ref[0, h]
            pen_h = lambda pg: pen[h * SUBLANES:(h + 1) * SUBLANES,
                                   pg * PAGE_SIZE:(pg + 1) * PAGE_SIZE]
            mx = jnp.full((SUBLANES, PAGE_SIZE), NEG_BIG, F32)
            for pg in range(n_pages):
                mx = jnp.maximum(mx, st_ref[pg, h] + pen_h(pg))
            m = mx.max(axis=1, keepdims=True)
            s_new = []
            for kk in range(n_new):
                sn = jnp.sum(qh * kn_ref[0, h, kk:kk + 1, :], axis=1, keepdims=True) * scale
                sn = jnp.where(q_idx >= kk, sn, NEG_BIG)
                s_new.append(sn)
                m = jnp.maximum(m, sn)
            l = jnp.zeros((SUBLANES, 1), F32)
            acc = jnp.zeros((SUBLANES, HEAD_DIM), F32)
            for kk in range(n_new):
                pn = jnp.exp(s_new[kk] - m)
                l = l + pn
                acc = acc + pn * vn_ref[0, h, kk:kk + 1, :]
            lsum = jnp.zeros((SUBLANES, PAGE_SIZE), F32)
            for pg in range(n_pages):
                pj = jnp.exp(st_ref[pg, h] + pen_h(pg) - m)
                lsum = lsum + pj
                st_ref[pg, h] = pj
            l_ref[h] = l + lsum.sum(axis=1, keepdims=True)
            acc_ref[h] = acc

    @pl.when(ph == 1)
    def _():
        for h in range(N_HEADS):
            acc = acc_ref[h]
            for i in range(pps):
                vh = head_rows(v_refs[i], h).astype(BF16)
                acc = acc + _mm(st_ref[s * pps + i, h].astype(BF16), vh)
            acc_ref[h] = acc

    @pl.when((ph == 1) & (s == n_steps - 1))
    def _():
        for h in range(N_HEADS):
            o_ref[0, h] = acc_ref[h] / l_ref[h]


def _moba_sample(q, k_new, v_new, cache_k, cache_v, page_table):
    bsz, n_new, _ = q.shape
    n_pages = page_table.shape[1]
    pps = SAMPLE_PAGES_PER_STEP
    assert n_new <= SUBLANES and n_pages % pps == 0 and pps % (MOBA_BLOCK // PAGE_SIZE) == 0
    assert n_pages * PAGE_SIZE // MOBA_BLOCK <= LANES
    n_steps = n_pages // pps

    def head_tiles(t):
        t = t.reshape(bsz, n_new, N_HEADS, HEAD_DIM).transpose(0, 2, 1, 3)
        return jnp.pad(t, ((0, 0), (0, 0), (0, SUBLANES - n_new), (0, 0)))

    def k_map(i):
        return lambda b, ph, s, pt: (
            pt[b * n_pages + jnp.where(ph == 0, s * pps + i, n_pages - pps + i)], 0, 0)

    def v_map(i):
        return lambda b, ph, s, pt: (pt[b * n_pages + jnp.where(ph == 1, s * pps + i, i)], 0, 0)

    tile = pl.BlockSpec((1, N_HEADS, SUBLANES, HEAD_DIM), lambda b, ph, s, pt: (b, 0, 0, 0))
    page = lambda index_map: pl.BlockSpec((1, PAGE_SIZE * N_HEADS, HEAD_DIM), index_map)
    grid_spec = pltpu.PrefetchScalarGridSpec(
        num_scalar_prefetch=1,
        grid=(bsz, 2, n_steps),
        in_specs=[tile, tile, tile] + [page(k_map(i)) for i in range(pps)]
        + [page(v_map(i)) for i in range(pps)],
        out_specs=tile,
        scratch_shapes=[pltpu.VMEM((n_pages, N_HEADS, SUBLANES, PAGE_SIZE), F32),
                        pltpu.VMEM((N_HEADS, LANES, HEAD_DIM), F32),
                        pltpu.VMEM((N_HEADS, SUBLANES, HEAD_DIM), F32),
                        pltpu.VMEM((N_HEADS, SUBLANES, 1), F32)])
    o = pl.pallas_call(
        functools.partial(_moba_sample_kernel, n_new=n_new, pps=pps),
        grid_spec=grid_spec,
        out_shape=jax.ShapeDtypeStruct((bsz, N_HEADS, SUBLANES, HEAD_DIM), F32),
        compiler_params=_params("parallel", "arbitrary", "arbitrary"),
        name="moba_sample",
    )(page_table.reshape(-1), head_tiles(q), head_tiles(k_new), head_tiles(v_new),
      *([cache_k] * pps), *([cache_v] * pps))
    return o[:, :, :n_new].transpose(0, 2, 1, 3).reshape(bsz, n_new, D_ATTN)


def _ssd_kernel(xbc_ref, z_ref, dtr_ref, conv0_ref, h0_ref, cw_ref, cb_ref, dtb_ref, alog_ref,
                dskip_ref, nw_ref, y_ref, hfin_ref, tail_ref, xx_ref, st_ref, yd_ref, *, valid_len):
    c = pl.program_id(1)
    nc = pl.num_programs(1)
    q = xbc_ref.shape[0]
    hp = SSM_HEAD_DIM
    gw = D_INNER // SSM_GROUPS
    heads_per_group = SSM_HEADS // SSM_GROUPS

    @pl.when(c == 0)
    def _():
        tail_ref[...] = conv0_ref[0]
        st_ref[...] = h0_ref[0].T

    xbc = xbc_ref[...]
    xx_ref[0:SUBLANES, :] = tail_ref[...]
    xx_ref[SUBLANES:SUBLANES + q, :] = xbc
    tail_ref[...] = xbc[q - SUBLANES:q, :]
    acc = cb_ref[...] + xx_ref[pl.ds(SUBLANES - CONV_WIDTH + 1, q), :] * cw_ref[0:1, :]
    for tap in range(1, CONV_WIDTH):
        acc = acc + xx_ref[pl.ds(SUBLANES - CONV_WIDTH + 1 + tap, q), :] * cw_ref[tap:tap + 1, :]
    xact = acc * _sigmoid(acc)
    xs = xact[:, :D_INNER]
    bm = xact[:, D_INNER:D_INNER + SSM_GROUPS * SSM_STATE]
    cm = xact[:, D_INNER + SSM_GROUPS * SSM_STATE:]

    v = dtr_ref[...] + dtb_ref[...]
    dt = jnp.maximum(v, 0.0) + jnp.log1p(jnp.exp(-jnp.abs(v)))
    if valid_len is not None:
        t_idx = c * q + lax.broadcasted_iota(jnp.int32, dt.shape, 0)
        dt = jnp.where(t_idx < valid_len, dt, 0.0)
    a = -jnp.exp(alog_ref[...])
    r_i = lax.broadcasted_iota(jnp.int32, (q, q), 0)
    c_i = lax.broadcasted_iota(jnp.int32, (q, q), 1)
    tril = c_i <= r_i
    acs = _mm_01(tril.astype(BF16), dt * a, 3, left=True)
    acs_last = acs[q - 1:q, :]
    acs_t = acs.T
    e_r = lax.broadcasted_iota(jnp.int32, (LANES, D_INNER), 0)
    e_c = lax.broadcasted_iota(jnp.int32, (LANES, D_INNER), 1)
    expand = (e_c // hp == e_r).astype(BF16)
    dt_e = _mm_01(expand, dt, 2)
    to_end_e = _mm_01(expand, jnp.exp(acs_last - acs) * dt, 2)
    eacs_e = _mm_01(expand, jnp.exp(acs), 3)
    cdec_e = _mm_01(expand, jnp.exp(acs_last), 3)

    xdt_b = (xs * dt_e).astype(BF16)
    xw_b = (xs * to_end_e).astype(BF16)
    for g in range(SSM_GROUPS):
        bg = bm[:, g * SSM_STATE:(g + 1) * SSM_STATE]
        cg_b = cm[:, g * SSM_STATE:(g + 1) * SSM_STATE].astype(BF16)
        cb = _nt(cg_b, bg.astype(BF16))
        for hh in range(heads_per_group):
            h = g * heads_per_group + hh
            seg = acs[:, h:h + 1] - acs_t[h:h + 1, :]
            decay = jnp.where(tril, jnp.exp(jnp.minimum(seg, 0.0)), 0.0)
            yd_ref[:, h * hp:(h + 1) * hp] = _mm((cb * decay).astype(BF16),
                                                 xdt_b[:, h * hp:(h + 1) * hp])
        gc = slice(g * gw, (g + 1) * gw)
        st_g = st_ref[:, gc]
        y_off = _mm(cg_b, st_g.astype(BF16)) * eacs_e[:, gc]
        yd_ref[:, gc] += y_off
        st_ref[:, gc] = st_g * cdec_e[:, gc] + _mm(bg.T.astype(BF16), xw_b[:, gc])

    y = yd_ref[...] + dskip_ref[...] * xs
    zz = z_ref[...]
    gated = y * (zz * _sigmoid(zz))
    for g in range(SSM_GROUPS):
        gc = slice(g * gw, (g + 1) * gw)
        gg = gated[:, gc]
        ms = jnp.mean(gg * gg, axis=1, keepdims=True)
        y_ref[:, gc] = gg * lax.rsqrt(ms + NORM_EPS) * nw_ref[:, gc]

    @pl.when(c == nc - 1)
    def _():
        hfin_ref[0] = st_ref[...].T


def _ssd(xbc, z, dtr, conv0, h0, conv_w, conv_b, dt_bias, a_log, d_skip, norm_w, bsz, seq,
         valid_len=None):
    q = min(SSD_CHUNK, seq)
    nc = seq // q
    pad16 = lambda t: jnp.pad(t.reshape(1, SSM_HEADS), ((0, 0), (0, DT_PAD - SSM_HEADS)))
    row = lambda w: pl.BlockSpec((q, w), lambda b, c: (b * nc + c, 0))
    const = lambda shape: pl.BlockSpec(shape, lambda b, c: (0,) * len(shape))
    per_b = lambda shape: pl.BlockSpec((1,) + shape, lambda b, c: (b, 0, 0))
    return pl.pallas_call(
        functools.partial(_ssd_kernel, valid_len=valid_len),
        grid=(bsz, nc),
        in_specs=[row(CONV_CH), row(D_INNER), row(DT_PAD),
                  per_b((SUBLANES, CONV_CH)), per_b((D_INNER, SSM_STATE)),
                  const((CONV_WIDTH, CONV_CH)), const((1, CONV_CH)),
                  const((1, DT_PAD)), const((1, DT_PAD)), const((1, D_INNER)),
                  const((1, D_INNER))],
        out_specs=[row(D_INNER), per_b((D_INNER, SSM_STATE))],
        out_shape=[jax.ShapeDtypeStruct((bsz * seq, D_INNER), F32),
                   jax.ShapeDtypeStruct((bsz, D_INNER, SSM_STATE), F32)],
        scratch_shapes=[pltpu.VMEM((SUBLANES, CONV_CH), F32),
                        pltpu.VMEM((q + SUBLANES, CONV_CH), F32),
                        pltpu.VMEM((SSM_STATE, D_INNER), F32),
                        pltpu.VMEM((q, D_INNER), F32)],
        compiler_params=_params("parallel", "arbitrary"),
        name="ssd",
    )(xbc, z, dtr, conv0, h0, conv_w, conv_b.reshape(1, CONV_CH), pad16(dt_bias), pad16(a_log),
      jnp.repeat(d_skip, SSM_HEAD_DIM).reshape(1, D_INNER), norm_w.reshape(1, D_INNER))


def _layer_norm(x, w, b):
    mu = jnp.mean(x, axis=-1, keepdims=True)
    xc = x - mu
    var = jnp.mean(xc * xc, axis=-1, keepdims=True)
    return xc * lax.rsqrt(var + NORM_EPS) * w + b


def _out_proj_kernel(attn_ref, ssd_ref, x_ref, w_ref, lnw_ref, lnb_ref, wr2_ref, wr1_ref, br_ref,
                     *rest):
    h_ref, hb_ref, logit_ref = rest[-3:]
    mixed = (_mm(attn_ref[...].astype(BF16), w_ref[0:D_ATTN, :])
             + _mm(ssd_ref[...].astype(BF16), w_ref[D_ATTN:, :]))
    h = _layer_norm(DEEPNORM_ALPHA * x_ref[...] + mixed, lnw_ref[...], lnb_ref[...])
    h_ref[...] = h
    h_hi = h.astype(BF16)
    hb_ref[...] = h_hi
    h_lo = (h - h_hi.astype(F32)).astype(BF16)
    part = _mm(h_hi, wr2_ref[...])
    logit_ref[...] = part[:, :LANES] + part[:, LANES:] + _mm(h_lo, wr1_ref[...]) + br_ref[...]


def _out_proj(attn, ssd, x, w_bf16, lw, tm, n_rows, row_offset=0, into=None):
    t = x.shape[0]
    assert row_offset % tm == 0 and t % tm == 0
    off = row_offset // tm
    n_in = t // tm
    steps = n_in if into is not None else pl.cdiv(n_rows, tm)
    row = lambda w: pl.BlockSpec((tm, w), lambda i: (jnp.minimum(i, n_in - 1), 0))
    out_row = lambda w: pl.BlockSpec((tm, w), lambda i: (off + i, 0))
    const = lambda shape: pl.BlockSpec(shape, lambda i: (0, 0))
    in_specs = [row(D_ATTN), row(D_INNER), row(D_MODEL), const((D_MODEL, D_MODEL)),
                const((1, D_MODEL)), const((1, D_MODEL)), const((D_MODEL, 2 * LANES)),
                const((D_MODEL, LANES)), const((1, LANES))]
    args = [attn, ssd, x, w_bf16, lw["ln1_w"].reshape(1, D_MODEL), lw["ln1_b"].reshape(1, D_MODEL),
            lw["w_router_hl"], lw["w_router_hi"], lw["b_router_pad"]]
    aliases = {}
    if into is not None:
        aliases = {len(args) + j: j for j in range(len(into))}
        in_specs += [pl.BlockSpec(memory_space=pl.ANY)] * len(into)
        args += list(into)
    return pl.pallas_call(
        _out_proj_kernel,
        grid=(steps,),
        in_specs=in_specs,
        out_specs=[out_row(D_MODEL), out_row(D_MODEL), out_row(LANES)],
        out_shape=[jax.ShapeDtypeStruct((n_rows, D_MODEL), F32),
                   jax.ShapeDtypeStruct((n_rows, D_MODEL), BF16),
                   jax.ShapeDtypeStruct((n_rows, LANES), F32)],
        input_output_aliases=aliases,
        compiler_params=_params("parallel"),
        name="out_proj",
    )(*args)


MOE_BM = 256
MOE_TF = 1024


MOE_HALF = MOE_BM // 2


def _row_cases(rows, compute, out_ref):
    @pl.when(rows > MOE_HALF)
    def _():
        compute(slice(0, MOE_BM))

    @pl.when((rows > 0) & (rows <= MOE_HALF))
    def _():
        compute(slice(0, MOE_HALF))
        out_ref[MOE_HALF:, :] = jnp.zeros((MOE_HALF, out_ref.shape[1]), out_ref.dtype)

    @pl.when(rows == 0)
    def _():
        out_ref[...] = jnp.zeros_like(out_ref)


def _gate_up_kernel(e_ref, n_ref, r_ref, first_ref, rows_ref, nxt_e_ref, nxt_n_ref, has_nxt_ref,
                    x_ref, w_hbm, bg_ref, bu_ref, act_ref, wg_stage, wu_stage, wgb_ref, wub_ref,
                    sem):
    i = pl.program_id(0)

    def tile_copies(e, n):
        gate = pltpu.make_async_copy(
            w_hbm.at[e, :, pl.ds(pl.multiple_of(n * MOE_TF, MOE_TF), MOE_TF)], wg_stage, sem.at[0])
        up = pltpu.make_async_copy(
            w_hbm.at[e, :, pl.ds(pl.multiple_of(D_FF + n * MOE_TF, MOE_TF), MOE_TF)], wu_stage,
            sem.at[1])
        return gate, up

    @pl.when(i == 0)
    def _():
        for c in tile_copies(e_ref[0], n_ref[0]):
            c.start()

    @pl.when(first_ref[i] == 1)
    def _():
        for c in tile_copies(e_ref[i], n_ref[i]):
            c.wait()
        wgb_ref[...] = wg_stage[...].astype(BF16)
        wub_ref[...] = wu_stage[...].astype(BF16)

        @pl.when(has_nxt_ref[i] == 1)
        def _():
            for c in tile_copies(nxt_e_ref[i], nxt_n_ref[i]):
                c.start()

    def compute(sl):
        x = x_ref[sl, :]
        g = jnp.minimum(_mm(x, wgb_ref[...]) + bg_ref[0], SWIGLU_LIMIT)
        u = jnp.clip(_mm(x, wub_ref[...]) + bu_ref[0], -SWIGLU_LIMIT, SWIGLU_LIMIT)
        act_ref[sl, :] = ((u + 1.0) * (g * _sigmoid(SWIGLU_ALPHA * g))).astype(BF16)

    _row_cases(rows_ref[i], compute, act_ref)


def _down_kernel(e_ref, first_ref, rows_ref, nxt_e_ref, has_nxt_ref, a_ref, w_hbm, bd_ref,
                 y_ref, wd_stage, wdb_ref, sem):
    i = pl.program_id(0)
    expert_copy = lambda e: pltpu.make_async_copy(w_hbm.at[e], wd_stage, sem.at[0])

    @pl.when(i == 0)
    def _():
        expert_copy(e_ref[0]).start()

    @pl.when(first_ref[i] == 1)
    def _():
        expert_copy(e_ref[i]).wait()
        wdb_ref[...] = wd_stage[...].astype(BF16)

        @pl.when(has_nxt_ref[i] == 1)
        def _():
            expert_copy(nxt_e_ref[i]).start()

    def compute(sl):
        y_ref[sl, :] = (_mm(a_ref[sl, :], wdb_ref[...]) + bd_ref[0]).astype(y_ref.dtype)

    _row_cases(rows_ref[i], compute, y_ref)


def _moe_experts(xs, sched, w_gate_up, b_gate_up, w_down, b_down):
    cap = xs.shape[0]
    n_blocks = cap // MOE_BM
    nt = D_FF // MOE_TF
    up_off = D_FF // MOE_TF
    b_gu = b_gate_up.reshape(N_EXPERTS, 1, 2 * D_FF)
    gu_spec = pltpu.PrefetchScalarGridSpec(
        num_scalar_prefetch=8,
        grid=(n_blocks * nt,),
        in_specs=[
            pl.BlockSpec((MOE_BM, D_MODEL), lambda i, e, n, r, *_: (r[i], 0)),
            pl.BlockSpec(memory_space=pl.ANY),
            pl.BlockSpec((1, 1, MOE_TF), lambda i, e, n, r, *_: (e[i], 0, n[i])),
            pl.BlockSpec((1, 1, MOE_TF), lambda i, e, n, r, *_: (e[i], 0, up_off + n[i])),
        ],
        out_specs=pl.BlockSpec((MOE_BM, MOE_TF), lambda i, e, n, r, *_: (r[i], n[i])),
        scratch_shapes=[pltpu.VMEM((D_MODEL, MOE_TF), F32), pltpu.VMEM((D_MODEL, MOE_TF), F32),
                        pltpu.VMEM((D_MODEL, MOE_TF), BF16), pltpu.VMEM((D_MODEL, MOE_TF), BF16),
                        pltpu.SemaphoreType.DMA((2,))])
    act = pl.pallas_call(
        _gate_up_kernel,
        grid_spec=gu_spec,
        out_shape=jax.ShapeDtypeStruct((cap, D_FF), BF16),
        compiler_params=_params("arbitrary"),
        name="moe_gate_up",
    )(sched["gu_e"], sched["gu_n"], sched["gu_r"], sched["gu_first"], sched["gu_rows"],
      sched["gu_nxt_e"], sched["gu_nxt_n"], sched["gu_has_nxt"], xs, w_gate_up, b_gu, b_gu)

    down_spec = pltpu.PrefetchScalarGridSpec(
        num_scalar_prefetch=5,
        grid=(n_blocks,),
        in_specs=[
            pl.BlockSpec((MOE_BM, D_FF), lambda i, e, *_: (i, 0)),
            pl.BlockSpec(memory_space=pl.ANY),
            pl.BlockSpec((1, 1, D_MODEL), lambda i, e, *_: (e[i], 0, 0)),
        ],
        out_specs=pl.BlockSpec((MOE_BM, D_MODEL), lambda i, e, *_: (i, 0)),
        scratch_shapes=[pltpu.VMEM((D_FF, D_MODEL), F32), pltpu.VMEM((D_FF, D_MODEL), BF16),
                        pltpu.SemaphoreType.DMA((1,))])
    return pl.pallas_call(
        _down_kernel,
        grid_spec=down_spec,
        out_shape=jax.ShapeDtypeStruct((cap, D_MODEL), BF16),
        compiler_params=_params("arbitrary"),
        name="moe_down",
    )(sched["d_e"], sched["d_first"], sched["d_rows"], sched["d_nxt_e"], sched["d_has_nxt"],
      act, w_down, b_down.reshape(N_EXPERTS, 1, D_MODEL))


def _route_kernel(logit_ref, idx_ref, gate_ref, pos_ref, cnt_ref, carry_ref):
    i = pl.program_id(0)
    tm = logit_ref.shape[0]

    @pl.when(i == 0)
    def _():
        carry_ref[...] = jnp.zeros_like(carry_ref)

    lane = lax.broadcasted_iota(jnp.int32, (tm, LANES), 1).astype(F32)
    lg = jnp.where(lane < N_EXPERTS, logit_ref[...], -jnp.inf)
    sel = jnp.zeros((tm, LANES), F32)
    vals, idxs = [], []
    for _ in range(TOP_K):
        mx = lg.max(axis=1, keepdims=True)
        ik = jnp.min(jnp.where(lg == mx, lane, float(LANES)), axis=1, keepdims=True)
        hit = lane == ik
        sel = sel + hit.astype(F32)
        lg = jnp.where(hit, -jnp.inf, lg)
        vals.append(mx)
        idxs.append(ik)
    ex = [jnp.exp(v - vals[0]) for v in vals]
    den = ex[0]
    for e in ex[1:]:
        den = den + e
    r_i = lax.broadcasted_iota(jnp.int32, (tm, tm), 0)
    c_i = lax.broadcasted_iota(jnp.int32, (tm, tm), 1)
    before = _mm((c_i < r_i).astype(BF16), sel.astype(BF16)) + carry_ref[...]
    idx_out = jnp.zeros((tm, LANES), F32)
    gate_out = jnp.zeros((tm, LANES), F32)
    pos_out = jnp.zeros((tm, LANES), F32)
    for kk in range(TOP_K):
        pos = jnp.sum(jnp.where(lane == idxs[kk], before, 0.0), axis=1, keepdims=True)
        idx_out = jnp.where(lane == kk, idxs[kk], idx_out)
        gate_out = jnp.where(lane == kk, ex[kk] / den, gate_out)
        pos_out = jnp.where(lane == kk, pos, pos_out)
    idx_ref[...] = idx_out.astype(jnp.int32)
    gate_ref[...] = gate_out
    pos_ref[...] = pos_out.astype(jnp.int32)
    carry_ref[...] += jnp.sum(sel, axis=0, keepdims=True)
    cnt_ref[...] = carry_ref[...].astype(jnp.int32)


def _route(logits, tm):
    t = logits.shape[0]
    row = pl.BlockSpec((tm, LANES), lambda i: (i, 0))
    return pl.pallas_call(
        _route_kernel,
        grid=(t // tm,),
        in_specs=[row],
        out_specs=[row, row, row, pl.BlockSpec((1, LANES), lambda i: (0, 0))],
        out_shape=[jax.ShapeDtypeStruct((t, LANES), jnp.int32),
                   jax.ShapeDtypeStruct((t, LANES), F32),
                   jax.ShapeDtypeStruct((t, LANES), jnp.int32),
                   jax.ShapeDtypeStruct((1, LANES), jnp.int32)],
        scratch_shapes=[pltpu.VMEM((1, LANES), F32)],
        compiler_params=_params("arbitrary"),
        name="moe_route",
    )(logits)


MOE_KEY_BITS = 17


def _moe_schedule(idx, pos, counts):
    i32 = jnp.int32
    n_tok = idx.shape[0]
    n_assign = n_tok * TOP_K
    low_mask = (1 << MOE_KEY_BITS) - 1
    assert n_assign < low_mask
    blocks_e = (counts + MOE_BM - 1) // MOE_BM
    blk_end = jnp.cumsum(blocks_e).astype(i32)
    blk_start = blk_end - blocks_e
    slot_of_assign = blk_start[idx] * MOE_BM + pos
    n_blocks = -(-n_assign // MOE_BM) + N_EXPERTS
    cap = n_blocks * MOE_BM
    count_ge = lambda x, ends: jnp.sum((x[:, None] >= ends[None, :]).astype(i32), axis=1)
    key_real = idx.reshape(-1) * (1 << MOE_KEY_BITS) + jnp.arange(n_assign, dtype=i32)
    pad_end = jnp.cumsum(blocks_e * MOE_BM - counts).astype(i32)
    pad_expert = count_ge(jnp.arange(cap - n_assign, dtype=i32), pad_end)
    key_pad = pad_expert * (1 << MOE_KEY_BITS) + low_mask
    low = jnp.sort(jnp.concatenate([key_real, key_pad])) & low_mask
    slot_tok = jnp.where(low == low_mask, 0, low // TOP_K)
    blk = jnp.arange(n_blocks, dtype=i32)
    blk_e = jnp.minimum(count_ge(blk, blk_end), N_EXPERTS - 1)
    blk_rows = jnp.clip(counts[blk_e] - (blk - blk_start[blk_e]) * MOE_BM, 0, MOE_BM)
    nblk_e = blocks_e.at[N_EXPERTS - 1].add(n_blocks - blk_end[-1])
    d_nxt = blk_start[blk_e] + nblk_e[blk_e]
    nt = D_FF // MOE_TF
    n_items = n_blocks * nt
    item = jnp.arange(n_items, dtype=i32)
    it_e = jnp.minimum(count_ge(item, jnp.cumsum(nblk_e * nt).astype(i32)), N_EXPERTS - 1)
    local_item = item - blk_start[it_e] * nt
    per = jnp.maximum(nblk_e[it_e], 1)
    it_n = local_item // per
    it_lr = local_item % per
    it_r = blk_start[it_e] + it_lr
    it_nxt = item - it_lr + per
    sched = {"gu_e": it_e, "gu_n": it_n, "gu_r": it_r, "gu_first": (it_lr == 0).astype(i32),
             "gu_rows": blk_rows[it_r],
             "gu_nxt_e": it_e[jnp.minimum(it_nxt, n_items - 1)],
             "gu_nxt_n": it_n[jnp.minimum(it_nxt, n_items - 1)],
             "gu_has_nxt": (it_nxt < n_items).astype(i32),
             "d_e": blk_e, "d_first": (blk == blk_start[blk_e]).astype(i32), "d_rows": blk_rows,
             "d_nxt_e": blk_e[jnp.minimum(d_nxt, n_blocks - 1)],
             "d_has_nxt": (d_nxt < n_blocks).astype(i32)}
    return slot_tok, slot_of_assign, sched


def _combine_kernel(y_ref, g_ref, h_ref, lnw_ref, lnb_ref, op_ref, os_ref, *, n_first):
    i = pl.program_id(0)
    g = g_ref[...]
    ff = y_ref[0].astype(F32) * g[:, 0:1]
    for kk in range(1, TOP_K):
        ff = ff + y_ref[kk].astype(F32) * g[:, kk:kk + 1]
    out = _layer_norm(DEEPNORM_ALPHA * h_ref[...] + ff, lnw_ref[...], lnb_ref[...])

    @pl.when(i < n_first)
    def _():
        op_ref[...] = out

    @pl.when(i >= n_first)
    def _():
        os_ref[...] = out


def _combine(y_assign, gates, h, ln_w, ln_b, n_p, tm):
    t = h.shape[0]
    assert n_p % tm == 0 and t % tm == 0
    n_first = n_p // tm
    row = lambda w: pl.BlockSpec((tm, w), lambda i: (i, 0))
    const = lambda shape: pl.BlockSpec(shape, lambda i: (0, 0))
    return pl.pallas_call(
        functools.partial(_combine_kernel, n_first=n_first),
        grid=(t // tm,),
        in_specs=[pl.BlockSpec((TOP_K, tm, D_MODEL), lambda i: (0, i, 0)), row(LANES),
                  row(D_MODEL), const((1, D_MODEL)), const((1, D_MODEL))],
        out_specs=[pl.BlockSpec((tm, D_MODEL), lambda i: (jnp.minimum(i, n_first - 1), 0)),
                   pl.BlockSpec((tm, D_MODEL), lambda i: (jnp.maximum(i - n_first, 0), 0))],
        out_shape=[jax.ShapeDtypeStruct((n_p, D_MODEL), F32),
                   jax.ShapeDtypeStruct((t - n_p, D_MODEL), F32)],
        compiler_params=_params("arbitrary"),
        name="moe_combine",
    )(y_assign, gates, h, ln_w.reshape(1, D_MODEL), ln_b.reshape(1, D_MODEL))


def _mixer(x, w_in_b, w_out_b, lw, attend, conv0, h0, bsz, seq, tm, n_rows, row_offset=0,
           into=None):
    q, k, v, z, xbc, dtr = _in_proj(x, w_in_b, tm)
    attn = attend(q, k, v)
    if seq % SSD_CHUNK == 0:
        ssd, h_fin = _ssd(xbc, z, dtr, conv0, h0, lw["conv_w"], lw["conv_b"], lw["dt_bias"],
                          lw["a_log"], lw["d_skip"], lw["ssm_norm_w"], bsz, seq)
    else:
        assert seq < SSD_CHUNK
        pad = lambda t: jnp.pad(t.reshape(bsz, seq, -1),
                                ((0, 0), (0, SSD_CHUNK - seq), (0, 0))).reshape(bsz * SSD_CHUNK, -1)
        ssd, h_fin = _ssd(pad(xbc), pad(z), pad(dtr), conv0, h0, lw["conv_w"], lw["conv_b"],
                          lw["dt_bias"], lw["a_log"], lw["d_skip"], lw["ssm_norm_w"], bsz,
                          SSD_CHUNK, valid_len=seq)
        ssd = ssd.reshape(bsz, SSD_CHUNK, D_INNER)[:, :seq].reshape(bsz * seq, D_INNER)
    bufs = _out_proj(attn, ssd, x, w_out_b, lw, tm, n_rows, row_offset, into)
    return bufs, k, v, xbc, h_fin


def kernel(x_prompt, x_sample, cache_k, cache_v, state_conv, state_ssm, page_table, w_in, conv_w,
           conv_b, dt_bias, a_log, d_skip, ssm_norm_w, w_out, ln1_w, ln1_b, w_router, b_router,
           w_gate_up, b_gate_up, w_down, b_down, ln2_w, ln2_b):
    depth = w_in.shape[0]
    assert depth == 1
    bsz, seq, _ = x_prompt.shape
    dbs, dseq, _ = x_sample.shape
    n_pool = cache_k.shape[1]
    tail = CONV_WIDTH - 1
    n_p, n_s = bsz * seq, dbs * dseq
    n_all = n_p + n_s
    wr = jnp.pad(w_router[0], ((0, 0), (0, LANES - N_EXPERTS)))
    wr_hi = wr.astype(BF16)
    wr_lo = (wr - wr_hi.astype(F32)).astype(BF16)
    lw = {"conv_w": conv_w[0], "conv_b": conv_b[0], "dt_bias": dt_bias[0], "a_log": a_log[0],
          "d_skip": d_skip[0], "ssm_norm_w": ssm_norm_w[0], "ln1_w": ln1_w[0], "ln1_b": ln1_b[0],
          "w_router_hl": jnp.concatenate([wr_hi, wr_lo], axis=1), "w_router_hi": wr_hi,
          "b_router_pad": jnp.pad(b_router[0], (0, LANES - N_EXPERTS)).reshape(1, LANES)}
    w_in_b = jnp.pad(w_in[0], ((0, 0), (0, DT_PAD - SSM_HEADS))).astype(BF16)
    w_out_b = w_out[0].astype(BF16)

    conv0_p = jnp.zeros((bsz, SUBLANES, CONV_CH), F32)
    h0_p = jnp.zeros((bsz, D_INNER, SSM_STATE), F32)
    bufs, kp, vp, xbc_p, hfin_p = _mixer(
        x_prompt.reshape(n_p, D_MODEL), w_in_b, w_out_b, lw,
        lambda q, k, v: _moba_prompt(q, k, v, bsz, seq), conv0_p, h0_p, bsz, seq, tm=256,
        n_rows=n_all)

    conv0_s = jnp.pad(state_conv[0], ((0, 0), (SUBLANES - tail, 0), (0, 0)))
    h0_s = state_ssm[0].reshape(dbs, D_INNER, SSM_STATE)
    ck = cache_k[0].reshape(n_pool, PAGE_SIZE * N_HEADS, HEAD_DIM)
    cv = cache_v[0].reshape(n_pool, PAGE_SIZE * N_HEADS, HEAD_DIM)

    def attend_sample(q, k, v):
        new = lambda t: t.reshape(dbs, dseq, D_ATTN)
        return _moba_sample(new(q), new(k), new(v), ck, cv, page_table).reshape(n_s, D_ATTN)

    (h1, h1b, logits), ks, vs, xbc_s, hfin_s = _mixer(
        x_sample.reshape(n_s, D_MODEL), w_in_b, w_out_b, lw, attend_sample, conv0_s, h0_s,
        dbs, dseq, tm=n_s, n_rows=n_all, row_offset=n_p, into=bufs)

    tm_c = 128
    tm_r = 3 * tm_c if n_all % (3 * tm_c) == 0 else tm_c
    idx, gates, pos, counts = _route(logits, tm_r)
    slot_tok, slot_of_assign, sched = _moe_schedule(idx[:, :TOP_K], pos[:, :TOP_K],
                                                    counts[0, :N_EXPERTS])
    yb = _moe_experts(h1b[slot_tok], sched, w_gate_up[0], b_gate_up[0], w_down[0], b_down[0])
    y_assign = yb[slot_of_assign.T.reshape(-1)].reshape(TOP_K, n_all, D_MODEL)
    out_p, out_s = _combine(y_assign, gates, h1, ln2_w[0], ln2_b[0], n_p, tm_c)

    y_prompt = out_p.reshape(bsz, seq, D_MODEL)
    y_sample = out_s.reshape(dbs, dseq, D_MODEL)
    heads = lambda t, b, s: t.reshape(1, b, s, N_HEADS, HEAD_DIM)
    k_prompt, v_prompt = heads(kp, bsz, seq), heads(vp, bsz, seq)
    conv_prompt = xbc_p.reshape(bsz, seq, CONV_CH)[:, seq - tail:][None]
    ssm_prompt = hfin_p.reshape(1, bsz, SSM_HEADS, SSM_HEAD_DIM, SSM_STATE)
    k_sample, v_sample = heads(ks, dbs, dseq), heads(vs, dbs, dseq)
    conv_sample = jnp.concatenate([state_conv[0], xbc_s.reshape(dbs, dseq, CONV_CH)],
                                  axis=1)[:, dseq:][None]
    ssm_sample = hfin_s.reshape(1, dbs, SSM_HEADS, SSM_HEAD_DIM, SSM_STATE)
    return (y_prompt, y_sample, k_prompt, v_prompt, conv_prompt, ssm_prompt,
            k_sample, v_sample, conv_sample, ssm_sample)
```

```python
import functools
import math

import jax
import jax.numpy as jnp
from jax import lax
from jax.experimental import pallas as pl
from jax.experimental.pallas import tpu as pltpu

F32 = jnp.float32
BF16 = jnp.bfloat16
HIGHEST = lax.Precision.HIGHEST

D_MODEL = 2048
N_HEADS = 8
HEAD_DIM = 128
D_ATTN = N_HEADS * HEAD_DIM
MOBA_BLOCK = 256
MOBA_TOPK = 3
D_INNER = 1024
SSM_HEAD_DIM = 64
SSM_HEADS = 16
SSM_GROUPS = 2
SSM_STATE = 128
CONV_WIDTH = 4
CONV_CH = D_INNER + 2 * SSM_GROUPS * SSM_STATE
SSD_CHUNK = 128
N_EXPERTS = 32
TOP_K = 4
D_FF = 2048
SWIGLU_LIMIT = 7.0
SWIGLU_ALPHA = 1.702
NORM_EPS = 1e-5
DEEPNORM_ALPHA = 2.0 ** 0.25
PAGE_SIZE = 128

LANES = 128
SUBLANES = 8
VMEM_LIMIT_BYTES = 56 * 1024 * 1024

DT_PAD = LANES
NEG_BIG = -1e30


def _params(*sem):
    return pltpu.CompilerParams(dimension_semantics=sem, vmem_limit_bytes=VMEM_LIMIT_BYTES)


def _nt(a, b, precision=None):
    return lax.dot_general(a, b, (((1,), (1,)), ((), ())), precision=precision,
                           preferred_element_type=F32)


def _mm(a, b, precision=None):
    return jnp.dot(a, b, precision=precision, preferred_element_type=F32)


def _mm_01(m01, x, pieces, left=False):
    out = None
    rest = x
    for _ in range(pieces):
        part = rest.astype(BF16)
        rest = rest - part.astype(F32)
        term = _mm(m01, part) if left else _mm(part, m01)
        out = term if out is None else out + term
    return out


def _sigmoid(x):
    return 1.0 / (1.0 + jnp.exp(-x))


def _in_proj_kernel(x_ref, w_ref, q_ref, k_ref, v_ref, z_ref, xbc_ref, dt_ref):
    xb = x_ref[...].astype(BF16)
    col = 0
    for ref in (q_ref, k_ref, v_ref, z_ref, xbc_ref, dt_ref):
        width = ref.shape[1]
        ref[...] = _mm(xb, w_ref[:, col:col + width])
        col += width


def _in_proj(x, w_bf16, tm):
    t = x.shape[0]
    widths = (D_ATTN, D_ATTN, D_ATTN, D_INNER, CONV_CH, DT_PAD)
    return pl.pallas_call(
        _in_proj_kernel,
        grid=(t // tm,),
        in_specs=[pl.BlockSpec((tm, D_MODEL), lambda i: (i, 0)),
                  pl.BlockSpec(w_bf16.shape, lambda i: (0, 0), pipeline_mode=pl.Buffered(1))],
        out_specs=[pl.BlockSpec((tm, w), lambda i: (i, 0)) for w in widths],
        out_shape=[jax.ShapeDtypeStruct((t, w), F32) for w in widths],
        compiler_params=_params("parallel"),
        name="in_proj",
    )(x, w_bf16)


def _moba_prompt_kernel(q_ref, k_ref, v_ref, o_ref):
    seq = q_ref.shape[0]
    nb = seq // MOBA_BLOCK
    assert nb <= SUBLANES
    scale = 1.0 / math.sqrt(HEAD_DIM)
    q = q_ref[...]
    k = k_ref[...]
    kmean = jnp.sum(k.reshape(nb, MOBA_BLOCK, HEAD_DIM), axis=1) * (1.0 / MOBA_BLOCK)
    kmean = jnp.concatenate([kmean, jnp.zeros((LANES - nb, HEAD_DIM), F32)], axis=0)
    gate = _nt(kmean, q, HIGHEST)[0:SUBLANES]
    own = lax.broadcasted_iota(jnp.int32, (SUBLANES, seq), 1) // MOBA_BLOCK
    blk = lax.broadcasted_iota(jnp.int32, (SUBLANES, seq), 0)
    rank = jnp.zeros((SUBLANES, seq), jnp.int32)
    for jj in range(nb):
        gj = gate[jj:jj + 1, :]
        ahead = ((gj > gate) | ((gj == gate) & (blk > jj))) & (own > jj)
        rank = rank + ahead.astype(jnp.int32)
    keep = ((blk < own) & (rank < MOBA_TOPK)) | (blk == own)
    pen = jnp.where(keep, 0.0, NEG_BIG)
    pen_t = jnp.concatenate([pen, jnp.zeros((LANES - SUBLANES, seq), F32)], axis=0).T
    lane = lax.broadcasted_iota(jnp.int32, (seq, LANES), 1)
    key_blk = lax.broadcasted_iota(jnp.int32, (seq, LANES), 0) // MOBA_BLOCK
    q_aug = jnp.concatenate([q.astype(BF16), pen_t.astype(BF16)], axis=1)
    k_aug = jnp.concatenate([k.astype(BF16), (lane == key_blk).astype(BF16)], axis=1)
    v_aug = jnp.concatenate([v_ref[...].astype(BF16), (lane == 0).astype(BF16)], axis=1)
    row = lax.broadcasted_iota(jnp.int32, (MOBA_BLOCK, MOBA_BLOCK), 0)
    col = lax.broadcasted_iota(jnp.int32, (MOBA_BLOCK, MOBA_BLOCK), 1)
    causal = col <= row
    for i in range(nb):
        rows = slice(i * MOBA_BLOCK, (i + 1) * MOBA_BLOCK)
        s_blocks = []
        for j in range(i + 1):
            s = _nt(q_aug[rows], k_aug[j * MOBA_BLOCK:(j + 1) * MOBA_BLOCK]) * scale
            if j == i:
                s = jnp.where(causal, s, NEG_BIG)
            s_blocks.append(s)
        mx = s_blocks[0]
        for s in s_blocks[1:]:
            mx = jnp.maximum(mx, s)
        m = mx.max(axis=1, keepdims=True)
        o = jnp.zeros((MOBA_BLOCK, 2 * HEAD_DIM), F32)
        for j, s in enumerate(s_blocks):
            p = jnp.exp(s - m).astype(BF16)
            o = o + _mm(p, v_aug[j * MOBA_BLOCK:(j + 1) * MOBA_BLOCK])
        o_ref[rows, :] = o[:, :HEAD_DIM] / o[:, HEAD_DIM:HEAD_DIM + 1]


def _moba_prompt(q, k, v, bsz, seq):
    spec = pl.BlockSpec((seq, HEAD_DIM), lambda b, h: (b, h))
    return pl.pallas_call(
        _moba_prompt_kernel,
        grid=(bsz, N_HEADS),
        in_specs=[spec, spec, spec],
        out_specs=spec,
        out_shape=jax.ShapeDtypeStruct((bsz * seq, D_ATTN), F32),
        compiler_params=_params("parallel", "parallel"),
        name="moba_prompt",
    )(q, k, v)


SAMPLE_PAGES_PER_STEP = 8


def _moba_sample_kernel(pt_ref, q_ref, kn_ref, vn_ref, *rest, n_new, pps):
    ck_hbm, cv_hbm, o_ref, st_ref, ksum_ref, acc_ref, l_ref, buf_ref, sem = rest
    b = pl.program_id(0)
    ph = pl.program_id(1)
    s = pl.program_id(2)
    n_steps = pl.num_programs(2)
    n_pages = st_ref.shape[0]
    ppb = MOBA_BLOCK // PAGE_SIZE
    n_blocks = n_pages // ppb
    scale = 1.0 / math.sqrt(HEAD_DIM)

    g = (b * 2 + ph) * n_steps + s
    slot = g % 2

    def page_copies(gg, slot_, act):
        b_ = gg // (2 * n_steps)
        ph_ = (gg // n_steps) % 2
        s_ = gg % n_steps
        for i in range(pps):
            page = pt_ref[b_ * n_pages + s_ * pps + i]
            for sweep, src in ((0, ck_hbm), (1, cv_hbm)):
                @pl.when(ph_ == sweep)
                def _():
                    act(pltpu.make_async_copy(src.at[page], buf_ref.at[slot_, i], sem.at[slot_, i]))

    @pl.when(g == 0)
    def _():
        page_copies(g, slot, lambda c: c.start())

    page_copies(g, slot, lambda c: c.wait())

    @pl.when(g + 1 < pl.num_programs(0) * 2 * n_steps)
    def _():
        page_copies(g + 1, 1 - slot, lambda c: c.start())

    k_refs = v_refs = [buf_ref.at[slot, i] for i in range(pps)]
    head_rows = lambda ref, h: ref[pl.ds(h, PAGE_SIZE, stride=N_HEADS), :]

    @pl.when((ph == 0) & (s == 0))
    def _():
        ksum_ref[...] = jnp.zeros_like(ksum_ref)

    @pl.when(ph == 0)
    def _():
        for bi in range(pps // ppb):
            ksum = jnp.zeros((N_HEADS, HEAD_DIM), F32)
            for i in range(bi * ppb, (bi + 1) * ppb):
                pg = s * pps + i
                ksum = ksum + jnp.sum(k_refs[i][...].reshape(PAGE_SIZE, N_HEADS, HEAD_DIM), axis=0)
                for h in range(N_HEADS):
                    kh = head_rows(k_refs[i], h).astype(BF16)
                    st_ref[pg, h] = _nt(q_ref[0, h].astype(BF16), kh) * scale
            blk = s * (pps // ppb) + bi
            for h in range(N_HEADS):
                ksum_ref[h, pl.ds(blk, 1), :] = ksum[h:h + 1, :]

    @pl.when((ph == 1) & (s == 0))
    def _():
        q_idx = lax.broadcasted_iota(jnp.int32, (SUBLANES, 1), 0)
        blk_id = lax.broadcasted_iota(jnp.int32, (SUBLANES, LANES), 1)
        pens = []
        for h in range(N_HEADS):
            gate = _nt(q_ref[0, h], ksum_ref[h] * (1.0 / MOBA_BLOCK), HIGHEST)
            rank = jnp.zeros(gate.shape, jnp.int32)
            for jj in range(n_blocks):
                gj = gate[:, jj:jj + 1]
                ahead = (gj > gate) | ((gj == gate) & (blk_id > jj))
                rank = rank + ahead.astype(jnp.int32)
            sel = (rank < MOBA_TOPK) & (blk_id < n_blocks)
            pens.append(jnp.where(sel, 0.0, NEG_BIG))
        n_keys = n_pages * PAGE_SIZE
        lane_blk = lax.broadcasted_iota(jnp.int32, (LANES, n_keys), 1) // MOBA_BLOCK
        row_blk = lax.broadcasted_iota(jnp.int32, (LANES, n_keys), 0)
        pen = _mm(jnp.concatenate(pens, axis=0).astype(BF16), (lane_blk == row_blk).astype(BF16))
        for h in range(N_HEADS):
            qh = q_ref[0, h]
            pen_h = lambda pg: pen[h * SUBLANES:(h + 1) * SUBLANES,
                                   pg * PAGE_SIZE:(pg + 1) * PAGE_SIZE]
            mx = jnp.full((SUBLANES, PAGE_SIZE), NEG_BIG, F32)
            for pg in range(n_pages):
                mx = jnp.maximum(mx, st_ref[pg, h] + pen_h(pg))
            m = mx.max(axis=1, keepdims=True)
            s_new = []
            for kk in range(n_new):
                sn = jnp.sum(qh * kn_ref[0, h, kk:kk + 1, :], axis=1, keepdims=True) * scale
                sn = jnp.where(q_idx >= kk, sn, NEG_BIG)
                s_new.append(sn)
                m = jnp.maximum(m, sn)
            l = jnp.zeros((SUBLANES, 1), F32)
            acc = jnp.zeros((SUBLANES, HEAD_DIM), F32)
            for kk in range(n_new):
                pn = jnp.exp(s_new[kk] - m)
                l = l + pn
                acc = acc + pn * vn_ref[0, h, kk:kk + 1, :]
            lsum = jnp.zeros((SUBLANES, PAGE_SIZE), F32)
            for pg in range(n_pages):
                pj = jnp.exp(st_ref[pg, h] + pen_h(pg) - m)
                lsum = lsum + pj
                st_ref[pg, h] = pj
            l_ref[h] = l + lsum.sum(axis=1, keepdims=True)
            acc_ref[h] = acc

    @pl.when(ph == 1)
    def _():
        for h in range(N_HEADS):
            acc = acc_ref[h]
            for i in range(pps):
                vh = head_rows(v_refs[i], h).astype(BF16)
                acc = acc + _mm(st_ref[s * pps + i, h].astype(BF16), vh)
            acc_ref[h] = acc

    @pl.when((ph == 1) & (s == n_steps - 1))
    def _():
        for h in range(N_HEADS):
            o_ref[0, h] = acc_ref[h] / l_ref[h]


def _moba_sample(q, k_new, v_new, cache_k, cache_v, page_table):
    bsz, n_new, _ = q.shape
    n_pages = page_table.shape[1]
    pps = SAMPLE_PAGES_PER_STEP
    assert n_new <= SUBLANES and n_pages % pps == 0 and pps % (MOBA_BLOCK // PAGE_SIZE) == 0
    assert n_pages * PAGE_SIZE // MOBA_BLOCK <= LANES
    n_steps = n_pages // pps

    def head_tiles(t):
        t = t.reshape(bsz, n_new, N_HEADS, HEAD_DIM).transpose(0, 2, 1, 3)
        return jnp.pad(t, ((0, 0), (0, 0), (0, SUBLANES - n_new), (0, 0)))

    tile = pl.BlockSpec((1, N_HEADS, SUBLANES, HEAD_DIM), lambda b, ph, s, pt: (b, 0, 0, 0))
    hbm = pl.BlockSpec(memory_space=pl.ANY)
    grid_spec = pltpu.PrefetchScalarGridSpec(
        num_scalar_prefetch=1,
        grid=(bsz, 2, n_steps),
        in_specs=[tile, tile, tile, hbm, hbm],
        out_specs=tile,
        scratch_shapes=[pltpu.VMEM((n_pages, N_HEADS, SUBLANES, PAGE_SIZE), F32),
                        pltpu.VMEM((N_HEADS, LANES, HEAD_DIM), F32),
                        pltpu.VMEM((N_HEADS, SUBLANES, HEAD_DIM), F32),
                        pltpu.VMEM((N_HEADS, SUBLANES, 1), F32),
                        pltpu.VMEM((2, pps, PAGE_SIZE * N_HEADS, HEAD_DIM), F32),
                        pltpu.SemaphoreType.DMA((2, pps))])
    o = pl.pallas_call(
        functools.partial(_moba_sample_kernel, n_new=n_new, pps=pps),
        grid_spec=grid_spec,
        out_shape=jax.ShapeDtypeStruct((bsz, N_HEADS, SUBLANES, HEAD_DIM), F32),
        compiler_params=_params("arbitrary", "arbitrary", "arbitrary"),
        name="moba_sample",
    )(page_table.reshape(-1), head_tiles(q), head_tiles(k_new), head_tiles(v_new),
      cache_k, cache_v)
    return o[:, :, :n_new].transpose(0, 2, 1, 3).reshape(bsz, n_new, D_ATTN)


def _ssd_kernel(xbc_ref, z_ref, dtr_ref, conv0_ref, h0_ref, cw_ref, cb_ref, dtb_ref, alog_ref,
                dskip_ref, nw_ref, y_ref, hfin_ref, tail_ref, xx_ref, st_ref, yd_ref, *, valid_len):
    c = pl.program_id(1)
    nc = pl.num_programs(1)
    q = xbc_ref.shape[0]
    hp = SSM_HEAD_DIM
    gw = D_INNER // SSM_GROUPS
    heads_per_group = SSM_HEADS // SSM_GROUPS

    @pl.when(c == 0)
    def _():
        tail_ref[...] = conv0_ref[0]
        st_ref[...] = h0_ref[0].T

    xbc = xbc_ref[...]
    xx_ref[0:SUBLANES, :] = tail_ref[...]
    xx_ref[SUBLANES:SUBLANES + q, :] = xbc
    tail_ref[...] = xbc[q - SUBLANES:q, :]
    acc = cb_ref[...] + xx_ref[pl.ds(SUBLANES - CONV_WIDTH + 1, q), :] * cw_ref[0:1, :]
    for tap in range(1, CONV_WIDTH):
        acc = acc + xx_ref[pl.ds(SUBLANES - CONV_WIDTH + 1 + tap, q), :] * cw_ref[tap:tap + 1, :]
    xact = acc * _sigmoid(acc)
    xs = xact[:, :D_INNER]
    bm = xact[:, D_INNER:D_INNER + SSM_GROUPS * SSM_STATE]
    cm = xact[:, D_INNER + SSM_GROUPS * SSM_STATE:]

    v = dtr_ref[...] + dtb_ref[...]
    dt = jnp.maximum(v, 0.0) + jnp.log1p(jnp.exp(-jnp.abs(v)))
    if valid_len is not None:
        t_idx = c * q + lax.broadcasted_iota(jnp.int32, dt.shape, 0)
        dt = jnp.where(t_idx < valid_len, dt, 0.0)
    a = -jnp.exp(alog_ref[...])
    r_i = lax.broadcasted_iota(jnp.int32, (q, q), 0)
    c_i = lax.broadcasted_iota(jnp.int32, (q, q), 1)
    tril = c_i <= r_i
    acs = _mm_01(tril.astype(BF16), dt * a, 3, left=True)
    acs_last = acs[q - 1:q, :]
    acs_t = acs.T
    e_r = lax.broadcasted_iota(jnp.int32, (LANES, D_INNER), 0)
    e_c = lax.broadcasted_iota(jnp.int32, (LANES, D_INNER), 1)
    expand = (e_c // hp == e_r).astype(BF16)
    dt_e = _mm_01(expand, dt, 2)
    to_end_e = _mm_01(expand, jnp.exp(acs_last - acs) * dt, 2)
    eacs_e = _mm_01(expand, jnp.exp(acs), 3)
    cdec_e = _mm_01(expand, jnp.exp(acs_last), 3)

    xdt_b = (xs * dt_e).astype(BF16)
    xw_b = (xs * to_end_e).astype(BF16)
    for g in range(SSM_GROUPS):
        bg = bm[:, g * SSM_STATE:(g + 1) * SSM_STATE]
        cg_b = cm[:, g * SSM_STATE:(g + 1) * SSM_STATE].astype(BF16)
        cb = _nt(cg_b, bg.astype(BF16))
        for hh in range(heads_per_group):
            h = g * heads_per_group + hh
            seg = acs[:, h:h + 1] - acs_t[h:h + 1, :]
            decay = jnp.where(tril, jnp.exp(jnp.minimum(seg, 0.0)), 0.0)
            yd_ref[:, h * hp:(h + 1) * hp] = _mm((cb * decay).astype(BF16),
                                                 xdt_b[:, h * hp:(h + 1) * hp])
        gc = slice(g * gw, (g + 1) * gw)
        st_g = st_ref[:, gc]
        y_off = _mm(cg_b, st_g.astype(BF16)) * eacs_e[:, gc]
        yd_ref[:, gc] += y_off
        st_ref[:, gc] = st_g * cdec_e[:, gc] + _mm(bg.T.astype(BF16), xw_b[:, gc])

    y = yd_ref[...] + dskip_ref[...] * xs
    zz = z_ref[...]
    gated = y * (zz * _sigmoid(zz))
    for g in range(SSM_GROUPS):
        gc = slice(g * gw, (g + 1) * gw)
        gg = gated[:, gc]
        ms = jnp.mean(gg * gg, axis=1, keepdims=True)
        y_ref[:, gc] = gg * lax.rsqrt(ms + NORM_EPS) * nw_ref[:, gc]

    @pl.when(c == nc - 1)
    def _():
        hfin_ref[0] = st_ref[...].T


def _ssd(xbc, z, dtr, conv0, h0, conv_w, conv_b, dt_bias, a_log, d_skip, norm_w, bsz, seq,
         valid_len=None):
    q = min(SSD_CHUNK, seq)
    nc = seq // q
    pad16 = lambda t: jnp.pad(t.reshape(1, SSM_HEADS), ((0, 0), (0, DT_PAD - SSM_HEADS)))
    row = lambda w: pl.BlockSpec((q, w), lambda b, c: (b * nc + c, 0))
    const = lambda shape: pl.BlockSpec(shape, lambda b, c: (0,) * len(shape))
    per_b = lambda shape: pl.BlockSpec((1,) + shape, lambda b, c: (b, 0, 0))
    return pl.pallas_call(
        functools.partial(_ssd_kernel, valid_len=valid_len),
        grid=(bsz, nc),
        in_specs=[row(CONV_CH), row(D_INNER), row(DT_PAD),
                  per_b((SUBLANES, CONV_CH)), per_b((D_INNER, SSM_STATE)),
                  const((CONV_WIDTH, CONV_CH)), const((1, CONV_CH)),
                  const((1, DT_PAD)), const((1, DT_PAD)), const((1, D_INNER)),
                  const((1, D_INNER))],
        out_specs=[row(D_INNER), per_b((D_INNER, SSM_STATE))],
        out_shape=[jax.ShapeDtypeStruct((bsz * seq, D_INNER), F32),
                   jax.ShapeDtypeStruct((bsz, D_INNER, SSM_STATE), F32)],
        scratch_shapes=[pltpu.VMEM((SUBLANES, CONV_CH), F32),
                        pltpu.VMEM((q + SUBLANES, CONV_CH), F32),
                        pltpu.VMEM((SSM_STATE, D_INNER), F32),
                        pltpu.VMEM((q, D_INNER), F32)],
        compiler_params=_params("parallel", "arbitrary"),
        name="ssd",
    )(xbc, z, dtr, conv0, h0, conv_w, conv_b.reshape(1, CONV_CH), pad16(dt_bias), pad16(a_log),
      jnp.repeat(d_skip, SSM_HEAD_DIM).reshape(1, D_INNER), norm_w.reshape(1, D_INNER))


def _layer_norm(x, w, b):
    mu = jnp.mean(x, axis=-1, keepdims=True)
    xc = x - mu
    var = jnp.mean(xc * xc, axis=-1, keepdims=True)
    return xc * lax.rsqrt(var + NORM_EPS) * w + b


def _out_proj_kernel(attn_ref, ssd_ref, x_ref, w_ref, lnw_ref, lnb_ref, wr2_ref, wr1_ref, br_ref,
                     *rest):
    h_ref, hb_ref, logit_ref = rest[-3:]
    mixed = (_mm(attn_ref[...].astype(BF16), w_ref[0:D_ATTN, :])
             + _mm(ssd_ref[...].astype(BF16), w_ref[D_ATTN:, :]))
    h = _layer_norm(DEEPNORM_ALPHA * x_ref[...] + mixed, lnw_ref[...], lnb_ref[...])
    h_ref[...] = h
    h_hi = h.astype(BF16)
    hb_ref[...] = h_hi
    h_lo = (h - h_hi.astype(F32)).astype(BF16)
    part = _mm(h_hi, wr2_ref[...])
    logit_ref[...] = part[:, :LANES] + part[:, LANES:] + _mm(h_lo, wr1_ref[...]) + br_ref[...]


def _out_proj(attn, ssd, x, w_bf16, lw, tm, n_rows, row_offset=0, into=None):
    t = x.shape[0]
    assert row_offset % tm == 0 and t % tm == 0
    off = row_offset // tm
    n_in = t // tm
    steps = n_in if into is not None else pl.cdiv(n_rows, tm)
    row = lambda w: pl.BlockSpec((tm, w), lambda i: (jnp.minimum(i, n_in - 1), 0))
    out_row = lambda w: pl.BlockSpec((tm, w), lambda i: (off + i, 0))
    const = lambda shape: pl.BlockSpec(shape, lambda i: (0, 0))
    in_specs = [row(D_ATTN), row(D_INNER), row(D_MODEL), const((D_MODEL, D_MODEL)),
                const((1, D_MODEL)), const((1, D_MODEL)), const((D_MODEL, 2 * LANES)),
                const((D_MODEL, LANES)), const((1, LANES))]
    args = [attn, ssd, x, w_bf16, lw["ln1_w"].reshape(1, D_MODEL), lw["ln1_b"].reshape(1, D_MODEL),
            lw["w_router_hl"], lw["w_router_hi"], lw["b_router_pad"]]
    aliases = {}
    if into is not None:
        aliases = {len(args) + j: j for j in range(len(into))}
        in_specs += [pl.BlockSpec(memory_space=pl.ANY)] * len(into)
        args += list(into)
    return pl.pallas_call(
        _out_proj_kernel,
        grid=(steps,),
        in_specs=in_specs,
        out_specs=[out_row(D_MODEL), out_row(D_MODEL), out_row(LANES)],
        out_shape=[jax.ShapeDtypeStruct((n_rows, D_MODEL), F32),
                   jax.ShapeDtypeStruct((n_rows, D_MODEL), BF16),
                   jax.ShapeDtypeStruct((n_rows, LANES), F32)],
        input_output_aliases=aliases,
        compiler_params=_params("parallel"),
        name="out_proj",
    )(*args)


MOE_BM = 512
MOE_TF = 1024


MOE_SUB = 128


def _row_cases(rows, compute, out_ref):
    for n in range(MOE_SUB, MOE_BM + 1, MOE_SUB):
        @pl.when((rows > n - MOE_SUB) & (rows <= n))
        def _():
            compute(slice(0, n))
            if n < MOE_BM:
                out_ref[n:, :] = jnp.zeros((MOE_BM - n, out_ref.shape[1]), out_ref.dtype)

    @pl.when(rows == 0)
    def _():
        out_ref[...] = jnp.zeros_like(out_ref)


def _gate_up_kernel(e_ref, n_ref, r_ref, first_ref, rows_ref, nxt_e_ref, nxt_n_ref, has_nxt_ref,
                    x_ref, w_hbm, bg_ref, bu_ref, act_ref, wg_stage, wu_stage, wgb_ref, wub_ref,
                    sem):
    i = pl.program_id(0)

    def tile_copies(e, n):
        gate = pltpu.make_async_copy(
            w_hbm.at[e, :, pl.ds(pl.multiple_of(n * MOE_TF, MOE_TF), MOE_TF)], wg_stage, sem.at[0])
        up = pltpu.make_async_copy(
            w_hbm.at[e, :, pl.ds(pl.multiple_of(D_FF + n * MOE_TF, MOE_TF), MOE_TF)], wu_stage,
            sem.at[1])
        return gate, up

    @pl.when(i == 0)
    def _():
        for c in tile_copies(e_ref[0], n_ref[0]):
            c.start()

    @pl.when(first_ref[i] == 1)
    def _():
        for c in tile_copies(e_ref[i], n_ref[i]):
            c.wait()
        wgb_ref[...] = wg_stage[...].astype(BF16)
        wub_ref[...] = wu_stage[...].astype(BF16)

        @pl.when(has_nxt_ref[i] == 1)
        def _():
            for c in tile_copies(nxt_e_ref[i], nxt_n_ref[i]):
                c.start()

    def compute(sl):
        x = x_ref[sl, :]
        g = jnp.minimum(_mm(x, wgb_ref[...]) + bg_ref[0], SWIGLU_LIMIT)
        u = jnp.clip(_mm(x, wub_ref[...]) + bu_ref[0], -SWIGLU_LIMIT, SWIGLU_LIMIT)
        act_ref[sl, :] = ((u + 1.0) * (g * _sigmoid(SWIGLU_ALPHA * g))).astype(BF16)

    _row_cases(rows_ref[i], compute, act_ref)


def _down_kernel(e_ref, first_ref, rows_ref, nxt_e_ref, has_nxt_ref, a_ref, w_hbm, bd_ref,
                 y_ref, wd_stage, wdb_ref, sem):
    i = pl.program_id(0)
    expert_copy = lambda e: pltpu.make_async_copy(w_hbm.at[e], wd_stage, sem.at[0])

    @pl.when(i == 0)
    def _():
        expert_copy(e_ref[0]).start()

    @pl.when(first_ref[i] == 1)
    def _():
        expert_copy(e_ref[i]).wait()
        wdb_ref[...] = wd_stage[...].astype(BF16)

        @pl.when(has_nxt_ref[i] == 1)
        def _():
            expert_copy(nxt_e_ref[i]).start()

    def compute(sl):
        y_ref[sl, :] = (_mm(a_ref[sl, :], wdb_ref[...]) + bd_ref[0]).astype(y_ref.dtype)

    _row_cases(rows_ref[i], compute, y_ref)


def _moe_experts(xs, sched, w_gate_up, b_gate_up, w_down, b_down):
    cap = xs.shape[0]
    n_blocks = cap // MOE_BM
    nt = D_FF // MOE_TF
    up_off = D_FF // MOE_TF
    b_gu = b_gate_up.reshape(N_EXPERTS, 1, 2 * D_FF)
    gu_spec = pltpu.PrefetchScalarGridSpec(
        num_scalar_prefetch=8,
        grid=(n_blocks * nt,),
        in_specs=[
            pl.BlockSpec((MOE_BM, D_MODEL), lambda i, e, n, r, *_: (r[i], 0)),
            pl.BlockSpec(memory_space=pl.ANY),
            pl.BlockSpec((1, 1, MOE_TF), lambda i, e, n, r, *_: (e[i], 0, n[i])),
            pl.BlockSpec((1, 1, MOE_TF), lambda i, e, n, r, *_: (e[i], 0, up_off + n[i])),
        ],
        out_specs=pl.BlockSpec((MOE_BM, MOE_TF), lambda i, e, n, r, *_: (r[i], n[i])),
        scratch_shapes=[pltpu.VMEM((D_MODEL, MOE_TF), F32), pltpu.VMEM((D_MODEL, MOE_TF), F32),
                        pltpu.VMEM((D_MODEL, MOE_TF), BF16), pltpu.VMEM((D_MODEL, MOE_TF), BF16),
                        pltpu.SemaphoreType.DMA((2,))])
    act = pl.pallas_call(
        _gate_up_kernel,
        grid_spec=gu_spec,
        out_shape=jax.ShapeDtypeStruct((cap, D_FF), BF16),
        compiler_params=_params("arbitrary"),
        name="moe_gate_up",
    )(sched["gu_e"], sched["gu_n"], sched["gu_r"], sched["gu_first"], sched["gu_rows"],
      sched["gu_nxt_e"], sched["gu_nxt_n"], sched["gu_has_nxt"], xs, w_gate_up, b_gu, b_gu)

    down_spec = pltpu.PrefetchScalarGridSpec(
        num_scalar_prefetch=5,
        grid=(n_blocks,),
        in_specs=[
            pl.BlockSpec((MOE_BM, D_FF), lambda i, e, *_: (i, 0)),
            pl.BlockSpec(memory_space=pl.ANY),
            pl.BlockSpec((1, 1, D_MODEL), lambda i, e, *_: (e[i], 0, 0)),
        ],
        out_specs=pl.BlockSpec((MOE_BM, D_MODEL), lambda i, e, *_: (i, 0)),
        scratch_shapes=[pltpu.VMEM((D_FF, D_MODEL), F32), pltpu.VMEM((D_FF, D_MODEL), BF16),
                        pltpu.SemaphoreType.DMA((1,))])
    return pl.pallas_call(
        _down_kernel,
        grid_spec=down_spec,
        out_shape=jax.ShapeDtypeStruct((cap, D_MODEL), BF16),
        compiler_params=_params("arbitrary"),
        name="moe_down",
    )(sched["d_e"], sched["d_first"], sched["d_rows"], sched["d_nxt_e"], sched["d_has_nxt"],
      act, w_down, b_down.reshape(N_EXPERTS, 1, D_MODEL))


def _route_kernel(logit_ref, idx_ref, gate_ref, pos_ref, cnt_ref, carry_ref):
    i = pl.program_id(0)
    tm = logit_ref.shape[0]

    @pl.when(i == 0)
    def _():
        carry_ref[...] = jnp.zeros_like(carry_ref)

    lane = lax.broadcasted_iota(jnp.int32, (tm, LANES), 1).astype(F32)
    lg = jnp.where(lane < N_EXPERTS, logit_ref[...], -jnp.inf)
    sel = jnp.zeros((tm, LANES), F32)
    vals, idxs = [], []
    for _ in range(TOP_K):
        mx = lg.max(axis=1, keepdims=True)
        ik = jnp.min(jnp.where(lg == mx, lane, float(LANES)), axis=1, keepdims=True)
        hit = lane == ik
        sel = sel + hit.astype(F32)
        lg = jnp.where(hit, -jnp.inf, lg)
        vals.append(mx)
        idxs.append(ik)
    ex = [jnp.exp(v - vals[0]) for v in vals]
    den = ex[0]
    for e in ex[1:]:
        den = den + e
    r_i = lax.broadcasted_iota(jnp.int32, (tm, tm), 0)
    c_i = lax.broadcasted_iota(jnp.int32, (tm, tm), 1)
    before = _mm((c_i < r_i).astype(BF16), sel.astype(BF16)) + carry_ref[...]
    idx_out = jnp.zeros((tm, LANES), F32)
    gate_out = jnp.zeros((tm, LANES), F32)
    pos_out = jnp.zeros((tm, LANES), F32)
    for kk in range(TOP_K):
        pos = jnp.sum(jnp.where(lane == idxs[kk], before, 0.0), axis=1, keepdims=True)
        idx_out = jnp.where(lane == kk, idxs[kk], idx_out)
        gate_out = jnp.where(lane == kk, ex[kk] / den, gate_out)
        pos_out = jnp.where(lane == kk, pos, pos_out)
    idx_ref[...] = idx_out.astype(jnp.int32)
    gate_ref[...] = gate_out
    pos_ref[...] = pos_out.astype(jnp.int32)
    carry_ref[...] += jnp.sum(sel, axis=0, keepdims=True)
    cnt_ref[...] = carry_ref[...].astype(jnp.int32)


def _route(logits, tm):
    t = logits.shape[0]
    row = pl.BlockSpec((tm, LANES), lambda i: (i, 0))
    return pl.pallas_call(
        _route_kernel,
        grid=(t // tm,),
        in_specs=[row],
        out_specs=[row, row, row, pl.BlockSpec((1, LANES), lambda i: (0, 0))],
        out_shape=[jax.ShapeDtypeStruct((t, LANES), jnp.int32),
                   jax.ShapeDtypeStruct((t, LANES), F32),
                   jax.ShapeDtypeStruct((t, LANES), jnp.int32),
                   jax.ShapeDtypeStruct((1, LANES), jnp.int32)],
        scratch_shapes=[pltpu.VMEM((1, LANES), F32)],
        compiler_params=_params("arbitrary"),
        name="moe_route",
    )(logits)


MOE_KEY_BITS = 17


def _moe_schedule(idx, pos, counts):
    i32 = jnp.int32
    n_tok = idx.shape[0]
    n_assign = n_tok * TOP_K
    low_mask = (1 << MOE_KEY_BITS) - 1
    assert n_assign < low_mask
    blocks_e = (counts + MOE_BM - 1) // MOE_BM
    blk_end = jnp.cumsum(blocks_e).astype(i32)
    blk_start = blk_end - blocks_e
    slot_of_assign = blk_start[idx] * MOE_BM + pos
    n_blocks = -(-n_assign // MOE_BM) + N_EXPERTS
    cap = n_blocks * MOE_BM
    count_ge = lambda x, ends: jnp.sum((x[:, None] >= ends[None, :]).astype(i32), axis=1)
    key_real = idx.reshape(-1) * (1 << MOE_KEY_BITS) + jnp.arange(n_assign, dtype=i32)
    pad_end = jnp.cumsum(blocks_e * MOE_BM - counts).astype(i32)
    pad_expert = count_ge(jnp.arange(cap - n_assign, dtype=i32), pad_end)
    key_pad = pad_expert * (1 << MOE_KEY_BITS) + low_mask
    low = jnp.sort(jnp.concatenate([key_real, key_pad])) & low_mask
    slot_tok = jnp.where(low == low_mask, 0, low // TOP_K)
    blk = jnp.arange(n_blocks, dtype=i32)
    blk_e = jnp.minimum(count_ge(blk, blk_end), N_EXPERTS - 1)
    blk_rows = jnp.clip(counts[blk_e] - (blk - blk_start[blk_e]) * MOE_BM, 0, MOE_BM)
    nblk_e = blocks_e.at[N_EXPERTS - 1].add(n_blocks - blk_end[-1])
    d_nxt = blk_start[blk_e] + nblk_e[blk_e]
    nt = D_FF // MOE_TF
    n_items = n_blocks * nt
    item = jnp.arange(n_items, dtype=i32)
    it_e = jnp.minimum(count_ge(item, jnp.cumsum(nblk_e * nt).astype(i32)), N_EXPERTS - 1)
    local_item = item - blk_start[it_e] * nt
    per = jnp.maximum(nblk_e[it_e], 1)
    it_n = local_item // per
    it_lr = local_item % per
    it_r = blk_start[it_e] + it_lr
    it_nxt = item - it_lr + per
    sched = {"gu_e": it_e, "gu_n": it_n, "gu_r": it_r, "gu_first": (it_lr == 0).astype(i32),
             "gu_rows": blk_rows[it_r],
             "gu_nxt_e": it_e[jnp.minimum(it_nxt, n_items - 1)],
             "gu_nxt_n": it_n[jnp.minimum(it_nxt, n_items - 1)],
             "gu_has_nxt": (it_nxt < n_items).astype(i32),
             "d_e": blk_e, "d_first": (blk == blk_start[blk_e]).astype(i32), "d_rows": blk_rows,
             "d_nxt_e": blk_e[jnp.minimum(d_nxt, n_blocks - 1)],
             "d_has_nxt": (d_nxt < n_blocks).astype(i32)}
    return slot_tok, slot_of_assign, sched


def _combine_kernel(y_ref, g_ref, h_ref, lnw_ref, lnb_ref, op_ref, os_ref, *, n_first):
    i = pl.program_id(0)
    g = g_ref[...]
    ff = y_ref[0].astype(F32) * g[:, 0:1]
    for kk in range(1, TOP_K):
        ff = ff + y_ref[kk].astype(F32) * g[:, kk:kk + 1]
    out = _layer_norm(DEEPNORM_ALPHA * h_ref[...] + ff, lnw_ref[...], lnb_ref[...])

    @pl.when(i < n_first)
    def _():
        op_ref[...] = out

    @pl.when(i >= n_first)
    def _():
        os_ref[...] = out


def _combine(y_assign, gates, h, ln_w, ln_b, n_p, tm):
    t = h.shape[0]
    assert n_p % tm == 0 and t % tm == 0
    n_first = n_p // tm
    row = lambda w: pl.BlockSpec((tm, w), lambda i: (i, 0))
    const = lambda shape: pl.BlockSpec(shape, lambda i: (0, 0))
    return pl.pallas_call(
        functools.partial(_combine_kernel, n_first=n_first),
        grid=(t // tm,),
        in_specs=[pl.BlockSpec((TOP_K, tm, D_MODEL), lambda i: (0, i, 0)), row(LANES),
                  row(D_MODEL), const((1, D_MODEL)), const((1, D_MODEL))],
        out_specs=[pl.BlockSpec((tm, D_MODEL), lambda i: (jnp.minimum(i, n_first - 1), 0)),
                   pl.BlockSpec((tm, D_MODEL), lambda i: (jnp.maximum(i - n_first, 0), 0))],
        out_shape=[jax.ShapeDtypeStruct((n_p, D_MODEL), F32),
                   jax.ShapeDtypeStruct((t - n_p, D_MODEL), F32)],
        compiler_params=_params("arbitrary"),
        name="moe_combine",
    )(y_assign, gates, h, ln_w.reshape(1, D_MODEL), ln_b.reshape(1, D_MODEL))


def _mixer(x, w_in_b, w_out_b, lw, attend, conv0, h0, bsz, seq, tm, n_rows, row_offset=0,
           into=None):
    q, k, v, z, xbc, dtr = _in_proj(x, w_in_b, tm)
    attn = attend(q, k, v)
    if seq % SSD_CHUNK == 0:
        ssd, h_fin = _ssd(xbc, z, dtr, conv0, h0, lw["conv_w"], lw["conv_b"], lw["dt_bias"],
                          lw["a_log"], lw["d_skip"], lw["ssm_norm_w"], bsz, seq)
    else:
        assert seq < SSD_CHUNK
        pad = lambda t: jnp.pad(t.reshape(bsz, seq, -1),
                                ((0, 0), (0, SSD_CHUNK - seq), (0, 0))).reshape(bsz * SSD_CHUNK, -1)
        ssd, h_fin = _ssd(pad(xbc), pad(z), pad(dtr), conv0, h0, lw["conv_w"], lw["conv_b"],
                          lw["dt_bias"], lw["a_log"], lw["d_skip"], lw["ssm_norm_w"], bsz,
                          SSD_CHUNK, valid_len=seq)
        ssd = ssd.reshape(bsz, SSD_CHUNK, D_INNER)[:, :seq].reshape(bsz * seq, D_INNER)
    bufs = _out_proj(attn, ssd, x, w_out_b, lw, tm, n_rows, row_offset, into)
    return bufs, k, v, xbc, h_fin


def kernel(x_prompt, x_sample, cache_k, cache_v, state_conv, state_ssm, page_table, w_in, conv_w,
           conv_b, dt_bias, a_log, d_skip, ssm_norm_w, w_out, ln1_w, ln1_b, w_router, b_router,
           w_gate_up, b_gate_up, w_down, b_down, ln2_w, ln2_b):
    depth = w_in.shape[0]
    assert depth == 1
    bsz, seq, _ = x_prompt.shape
    dbs, dseq, _ = x_sample.shape
    n_pool = cache_k.shape[1]
    tail = CONV_WIDTH - 1
    n_p, n_s = bsz * seq, dbs * dseq
    n_all = n_p + n_s
    wr = jnp.pad(w_router[0], ((0, 0), (0, LANES - N_EXPERTS)))
    wr_hi = wr.astype(BF16)
    wr_lo = (wr - wr_hi.astype(F32)).astype(BF16)
    lw = {"conv_w": conv_w[0], "conv_b": conv_b[0], "dt_bias": dt_bias[0], "a_log": a_log[0],
          "d_skip": d_skip[0], "ssm_norm_w": ssm_norm_w[0], "ln1_w": ln1_w[0], "ln1_b": ln1_b[0],
          "w_router_hl": jnp.concatenate([wr_hi, wr_lo], axis=1), "w_router_hi": wr_hi,
          "b_router_pad": jnp.pad(b_router[0], (0, LANES - N_EXPERTS)).reshape(1, LANES)}
    w_in_b = jnp.pad(w_in[0], ((0, 0), (0, DT_PAD - SSM_HEADS))).astype(BF16)
    w_out_b = w_out[0].astype(BF16)

    conv0_p = jnp.zeros((bsz, SUBLANES, CONV_CH), F32)
    h0_p = jnp.zeros((bsz, D_INNER, SSM_STATE), F32)
    bufs, kp, vp, xbc_p, hfin_p = _mixer(
        x_prompt.reshape(n_p, D_MODEL), w_in_b, w_out_b, lw,
        lambda q, k, v: _moba_prompt(q, k, v, bsz, seq), conv0_p, h0_p, bsz, seq, tm=256,
        n_rows=n_all)

    conv0_s = jnp.pad(state_conv[0], ((0, 0), (SUBLANES - tail, 0), (0, 0)))
    h0_s = state_ssm[0].reshape(dbs, D_INNER, SSM_STATE)
    ck = cache_k[0].reshape(n_pool, PAGE_SIZE * N_HEADS, HEAD_DIM)
    cv = cache_v[0].reshape(n_pool, PAGE_SIZE * N_HEADS, HEAD_DIM)

    def attend_sample(q, k, v):
        new = lambda t: t.reshape(dbs, dseq, D_ATTN)
        return _moba_sample(new(q), new(k), new(v), ck, cv, page_table).reshape(n_s, D_ATTN)

    (h1, h1b, logits), ks, vs, xbc_s, hfin_s = _mixer(
        x_sample.reshape(n_s, D_MODEL), w_in_b, w_out_b, lw, attend_sample, conv0_s, h0_s,
        dbs, dseq, tm=n_s, n_rows=n_all, row_offset=n_p, into=bufs)

    tm_c = 128
    tm_r = 3 * tm_c if n_all % (3 * tm_c) == 0 else tm_c
    idx, gates, pos, counts = _route(logits, tm_r)
    slot_tok, slot_of_assign, sched = _moe_schedule(idx[:, :TOP_K], pos[:, :TOP_K],
                                                    counts[0, :N_EXPERTS])
    yb = _moe_experts(h1b[slot_tok], sched, w_gate_up[0], b_gate_up[0], w_down[0], b_down[0])
    y_assign = yb[slot_of_assign.T.reshape(-1)].reshape(TOP_K, n_all, D_MODEL)
    out_p, out_s = _combine(y_assign, gates, h1, ln2_w[0], ln2_b[0], n_p, tm_c)

    y_prompt = out_p.reshape(bsz, seq, D_MODEL)
    y_sample = out_s.reshape(dbs, dseq, D_MODEL)
    heads = lambda t, b, s: t.reshape(1, b, s, N_HEADS, HEAD_DIM)
    k_prompt, v_prompt = heads(kp, bsz, seq), heads(vp, bsz, seq)
    conv_prompt = xbc_p.reshape(bsz, seq, CONV_CH)[:, seq - tail:][None]
    ssm_prompt = hfin_p.reshape(1, bsz, SSM_HEADS, SSM_HEAD_DIM, SSM_STATE)
    k_sample, v_sample = heads(ks, dbs, dseq), heads(vs, dbs, dseq)
    conv_sample = jnp.concatenate([state_conv[0], xbc_s.reshape(dbs, dseq, CONV_CH)],
                                  axis=1)[:, dseq:][None]
    ssm_sample = hfin_s.reshape(1, dbs, SSM_HEADS, SSM_HEAD_DIM, SSM_STATE)
    return (y_prompt, y_sample, k_prompt, v_prompt, conv_prompt, ssm_prompt,
            k_sample, v_sample, conv_sample, ssm_sample)
```

```python
import functools
import math

import jax
import jax.numpy as jnp
from jax import lax
from jax.experimental import pallas as pl
from jax.experimental.pallas import tpu as pltpu

F32 = jnp.float32
BF16 = jnp.bfloat16
HIGHEST = lax.Precision.HIGHEST

D_MODEL = 2048
N_HEADS = 8
HEAD_DIM = 128
D_ATTN = N_HEADS * HEAD_DIM
MOBA_BLOCK = 256
MOBA_TOPK = 3
D_INNER = 1024
SSM_HEAD_DIM = 64
SSM_HEADS = 16
SSM_GROUPS = 2
SSM_STATE = 128
CONV_WIDTH = 4
CONV_CH = D_INNER + 2 * SSM_GROUPS * SSM_STATE
SSD_CHUNK = 128
N_EXPERTS = 32
TOP_K = 4
D_FF = 2048
SWIGLU_LIMIT = 7.0
SWIGLU_ALPHA = 1.702
NORM_EPS = 1e-5
DEEPNORM_ALPHA = 2.0 ** 0.25
PAGE_SIZE = 128

LANES = 128
SUBLANES = 8
VMEM_LIMIT_BYTES = 56 * 1024 * 1024

DT_PAD = LANES
NEG_BIG = -1e30


def _params(*sem):
    return pltpu.CompilerParams(dimension_semantics=sem, vmem_limit_bytes=VMEM_LIMIT_BYTES)


def _nt(a, b, precision=None):
    return lax.dot_general(a, b, (((1,), (1,)), ((), ())), precision=precision,
                           preferred_element_type=F32)


def _mm(a, b, precision=None):
    return jnp.dot(a, b, precision=precision, preferred_element_type=F32)


def _mm_01(m01, x, pieces, left=False):
    out = None
    rest = x
    for _ in range(pieces):
        part = rest.astype(BF16)
        rest = rest - part.astype(F32)
        term = _mm(m01, part) if left else _mm(part, m01)
        out = term if out is None else out + term
    return out


def _sigmoid(x):
    return 1.0 / (1.0 + jnp.exp(-x))


def _in_proj_kernel(x_ref, w_ref, q_ref, k_ref, v_ref, z_ref, xbc_ref, dt_ref):
    xb = x_ref[...].astype(BF16)
    col = 0
    for ref in (q_ref, k_ref, v_ref, z_ref, xbc_ref, dt_ref):
        width = ref.shape[1]
        ref[...] = _mm(xb, w_ref[:, col:col + width])
        col += width


def _in_proj(x, w_bf16, tm):
    t = x.shape[0]
    widths = (D_ATTN, D_ATTN, D_ATTN, D_INNER, CONV_CH, DT_PAD)
    return pl.pallas_call(
        _in_proj_kernel,
        grid=(t // tm,),
        in_specs=[pl.BlockSpec((tm, D_MODEL), lambda i: (i, 0)),
                  pl.BlockSpec(w_bf16.shape, lambda i: (0, 0), pipeline_mode=pl.Buffered(1))],
        out_specs=[pl.BlockSpec((tm, w), lambda i: (i, 0)) for w in widths],
        out_shape=[jax.ShapeDtypeStruct((t, w), F32) for w in widths],
        compiler_params=_params("parallel"),
        name="in_proj",
    )(x, w_bf16)


def _moba_prompt_kernel(q_ref, k_ref, v_ref, o_ref):
    seq = q_ref.shape[0]
    nb = seq // MOBA_BLOCK
    assert nb <= SUBLANES
    scale = 1.0 / math.sqrt(HEAD_DIM)
    q = q_ref[...]
    k = k_ref[...]
    kmean = jnp.sum(k.reshape(nb, MOBA_BLOCK, HEAD_DIM), axis=1) * (1.0 / MOBA_BLOCK)
    kmean = jnp.concatenate([kmean, jnp.zeros((LANES - nb, HEAD_DIM), F32)], axis=0)
    gate = _nt(kmean, q, HIGHEST)[0:SUBLANES]
    own = lax.broadcasted_iota(jnp.int32, (SUBLANES, seq), 1) // MOBA_BLOCK
    blk = lax.broadcasted_iota(jnp.int32, (SUBLANES, seq), 0)
    rank = jnp.zeros((SUBLANES, seq), jnp.int32)
    for jj in range(nb):
        gj = gate[jj:jj + 1, :]
        ahead = ((gj > gate) | ((gj == gate) & (blk > jj))) & (own > jj)
        rank = rank + ahead.astype(jnp.int32)
    keep = ((blk < own) & (rank < MOBA_TOPK)) | (blk == own)
    pen = jnp.where(keep, 0.0, NEG_BIG)
    pen_t = jnp.concatenate([pen, jnp.zeros((LANES - SUBLANES, seq), F32)], axis=0).T
    lane = lax.broadcasted_iota(jnp.int32, (seq, LANES), 1)
    key_blk = lax.broadcasted_iota(jnp.int32, (seq, LANES), 0) // MOBA_BLOCK
    q_aug = jnp.concatenate([q.astype(BF16), pen_t.astype(BF16)], axis=1)
    k_aug = jnp.concatenate([k.astype(BF16), (lane == key_blk).astype(BF16)], axis=1)
    v_aug = jnp.concatenate([v_ref[...].astype(BF16), (lane == 0).astype(BF16)], axis=1)
    row = lax.broadcasted_iota(jnp.int32, (MOBA_BLOCK, MOBA_BLOCK), 0)
    col = lax.broadcasted_iota(jnp.int32, (MOBA_BLOCK, MOBA_BLOCK), 1)
    causal = col <= row
    for i in range(nb):
        rows = slice(i * MOBA_BLOCK, (i + 1) * MOBA_BLOCK)
        s_blocks = []
        for j in range(i + 1):
            s = _nt(q_aug[rows], k_aug[j * MOBA_BLOCK:(j + 1) * MOBA_BLOCK]) * scale
            if j == i:
                s = jnp.where(causal, s, NEG_BIG)
            s_blocks.append(s)
        mx = s_blocks[0]
        for s in s_blocks[1:]:
            mx = jnp.maximum(mx, s)
        m = mx.max(axis=1, keepdims=True)
        o = jnp.zeros((MOBA_BLOCK, 2 * HEAD_DIM), F32)
        for j, s in enumerate(s_blocks):
            p = jnp.exp(s - m).astype(BF16)
            o = o + _mm(p, v_aug[j * MOBA_BLOCK:(j + 1) * MOBA_BLOCK])
        o_ref[rows, :] = o[:, :HEAD_DIM] / o[:, HEAD_DIM:HEAD_DIM + 1]


def _moba_prompt(q, k, v, bsz, seq):
    spec = pl.BlockSpec((seq, HEAD_DIM), lambda b, h: (b, h))
    return pl.pallas_call(
        _moba_prompt_kernel,
        grid=(bsz, N_HEADS),
        in_specs=[spec, spec, spec],
        out_specs=spec,
        out_shape=jax.ShapeDtypeStruct((bsz * seq, D_ATTN), F32),
        compiler_params=_params("parallel", "parallel"),
        name="moba_prompt",
    )(q, k, v)


SAMPLE_PAGES_PER_STEP = 8


def _moba_sample_kernel(pt_ref, q_ref, kn_ref, vn_ref, *rest, n_new, pps):
    del pt_ref
    k_refs, v_refs = rest[:pps], rest[pps:2 * pps]
    o_ref, st_ref, ksum_ref, acc_ref, l_ref = rest[2 * pps:]
    ph = pl.program_id(1)
    s = pl.program_id(2)
    n_steps = pl.num_programs(2)
    n_pages = st_ref.shape[0]
    ppb = MOBA_BLOCK // PAGE_SIZE
    n_blocks = n_pages // ppb
    scale = 1.0 / math.sqrt(HEAD_DIM)
    head_rows = lambda ref, h: ref[0, pl.ds(h, PAGE_SIZE, stride=N_HEADS), :]

    @pl.when((ph == 0) & (s == 0))
    def _():
        ksum_ref[...] = jnp.zeros_like(ksum_ref)

    @pl.when(ph == 0)
    def _():
        for bi in range(pps // ppb):
            ksum = jnp.zeros((N_HEADS, HEAD_DIM), F32)
            for i in range(bi * ppb, (bi + 1) * ppb):
                pg = s * pps + i
                ksum = ksum + jnp.sum(k_refs[i][0].reshape(PAGE_SIZE, N_HEADS, HEAD_DIM), axis=0)
                for h in range(N_HEADS):
                    kh = head_rows(k_refs[i], h).astype(BF16)
                    st_ref[pg, h] = _nt(q_ref[0, h].astype(BF16), kh) * scale
            blk = s * (pps // ppb) + bi
            for h in range(N_HEADS):
                ksum_ref[h, pl.ds(blk, 1), :] = ksum[h:h + 1, :]

    @pl.when((ph == 1) & (s == 0))
    def _():
        q_idx = lax.broadcasted_iota(jnp.int32, (SUBLANES, 1), 0)
        blk_id = lax.broadcasted_iota(jnp.int32, (SUBLANES, LANES), 1)
        pens = []
        for h in range(N_HEADS):
            gate = _nt(q_ref[0, h], ksum_ref[h] * (1.0 / MOBA_BLOCK), HIGHEST)
            rank = jnp.zeros(gate.shape, jnp.int32)
            for jj in range(n_blocks):
                gj = gate[:, jj:jj + 1]
                ahead = (gj > gate) | ((gj == gate) & (blk_id > jj))
                rank = rank + ahead.astype(jnp.int32)
            sel = (rank < MOBA_TOPK) & (blk_id < n_blocks)
            pens.append(jnp.where(sel, 0.0, NEG_BIG))
        n_keys = n_pages * PAGE_SIZE
        lane_blk = lax.broadcasted_iota(jnp.int32, (LANES, n_keys), 1) // MOBA_BLOCK
        row_blk = lax.broadcasted_iota(jnp.int32, (LANES, n_keys), 0)
        pen = _mm(jnp.concatenate(pens, axis=0).astype(BF16), (lane_blk == row_blk).astype(BF16))
        for h in range(N_HEADS):
            qh = q_ref[0, h]
            pen_h = lambda pg: pen[h * SUBLANES:(h + 1) * SUBLANES,
                                   pg * PAGE_SIZE:(pg + 1) * PAGE_SIZE]
            mx = jnp.full((SUBLANES, PAGE_SIZE), NEG_BIG, F32)
            for pg in range(n_pages):
                mx = jnp.maximum(mx, st_ref[pg, h] + pen_h(pg))
            m = mx.max(axis=1, keepdims=True)
            s_new = []
            for kk in range(n_new):
                sn = jnp.sum(qh * kn_ref[0, h, kk:kk + 1, :], axis=1, keepdims=True) * scale
                sn = jnp.where(q_idx >= kk, sn, NEG_BIG)
                s_new.append(sn)
                m = jnp.maximum(m, sn)
            l = jnp.zeros((SUBLANES, 1), F32)
            acc = jnp.zeros((SUBLANES, HEAD_DIM), F32)
            for kk in range(n_new):
                pn = jnp.exp(s_new[kk] - m)
                l = l + pn
                acc = acc + pn * vn_ref[0, h, kk:kk + 1, :]
            lsum = jnp.zeros((SUBLANES, PAGE_SIZE), F32)
            for pg in range(n_pages):
                pj = jnp.exp(st_ref[pg, h] + pen_h(pg) - m)
                lsum = lsum + pj
                st_ref[pg, h] = pj
            l_ref[h] = l + lsum.sum(axis=1, keepdims=True)
            acc_ref[h] = acc

    @pl.when(ph == 1)
    def _():
        for h in range(N_HEADS):
            acc = acc_ref[h]
            for i in range(pps):
                vh = head_rows(v_refs[i], h).astype(BF16)
                acc = acc + _mm(st_ref[s * pps + i, h].astype(BF16), vh)
            acc_ref[h] = acc

    @pl.when((ph == 1) & (s == n_steps - 1))
    def _():
        for h in range(N_HEADS):
            o_ref[0, h] = acc_ref[h] / l_ref[h]


def _moba_sample(q, k_new, v_new, cache_k, cache_v, page_table):
    bsz, n_new, _ = q.shape
    n_pages = page_table.shape[1]
    pps = SAMPLE_PAGES_PER_STEP
    assert n_new <= SUBLANES and n_pages % pps == 0 and pps % (MOBA_BLOCK // PAGE_SIZE) == 0
    assert n_pages * PAGE_SIZE // MOBA_BLOCK <= LANES
    n_steps = n_pages // pps

    def head_tiles(t):
        t = t.reshape(bsz, n_new, N_HEADS, HEAD_DIM).transpose(0, 2, 1, 3)
        return jnp.pad(t, ((0, 0), (0, 0), (0, SUBLANES - n_new), (0, 0)))

    def k_map(i):
        return lambda b, ph, s, pt: (
            pt[b * n_pages + jnp.where(ph == 0, s * pps + i, n_pages - pps + i)], 0, 0)

    def v_map(i):
        return lambda b, ph, s, pt: (pt[b * n_pages + jnp.where(ph == 1, s * pps + i, i)], 0, 0)

    tile = pl.BlockSpec((1, N_HEADS, SUBLANES, HEAD_DIM), lambda b, ph, s, pt: (b, 0, 0, 0))
    page = lambda index_map: pl.BlockSpec((1, PAGE_SIZE * N_HEADS, HEAD_DIM), index_map)
    grid_spec = pltpu.PrefetchScalarGridSpec(
        num_scalar_prefetch=1,
        grid=(bsz, 2, n_steps),
        in_specs=[tile, tile, tile] + [page(k_map(i)) for i in range(pps)]
        + [page(v_map(i)) for i in range(pps)],
        out_specs=tile,
        scratch_shapes=[pltpu.VMEM((n_pages, N_HEADS, SUBLANES, PAGE_SIZE), F32),
                        pltpu.VMEM((N_HEADS, LANES, HEAD_DIM), F32),
                        pltpu.VMEM((N_HEADS, SUBLANES, HEAD_DIM), F32),
                        pltpu.VMEM((N_HEADS, SUBLANES, 1), F32)])
    o = pl.pallas_call(
        functools.partial(_moba_sample_kernel, n_new=n_new, pps=pps),
        grid_spec=grid_spec,
        out_shape=jax.ShapeDtypeStruct((bsz, N_HEADS, SUBLANES, HEAD_DIM), F32),
        compiler_params=_params("parallel", "arbitrary", "arbitrary"),
        name="moba_sample",
    )(page_table.reshape(-1), head_tiles(q), head_tiles(k_new), head_tiles(v_new),
      *([cache_k] * pps), *([cache_v] * pps))
    return o[:, :, :n_new].transpose(0, 2, 1, 3).reshape(bsz, n_new, D_ATTN)


def _ssd_kernel(xbc_ref, z_ref, dtr_ref, conv0_ref, h0_ref, cw_ref, cb_ref, dtb_ref, alog_ref,
                dskip_ref, nw_ref, y_ref, hfin_ref, tail_ref, xx_ref, st_ref, yd_ref, *, valid_len):
    c = pl.program_id(1)
    nc = pl.num_programs(1)
    q = xbc_ref.shape[0]
    hp = SSM_HEAD_DIM
    gw = D_INNER // SSM_GROUPS
    heads_per_group = SSM_HEADS // SSM_GROUPS

    @pl.when(c == 0)
    def _():
        tail_ref[...] = conv0_ref[0]
        st_ref[...] = h0_ref[0].T

    xbc = xbc_ref[...]
    xx_ref[0:SUBLANES, :] = tail_ref[...]
    xx_ref[SUBLANES:SUBLANES + q, :] = xbc
    tail_ref[...] = xbc[q - SUBLANES:q, :]
    acc = cb_ref[...] + xx_ref[pl.ds(SUBLANES - CONV_WIDTH + 1, q), :] * cw_ref[0:1, :]
    for tap in range(1, CONV_WIDTH):
        acc = acc + xx_ref[pl.ds(SUBLANES - CONV_WIDTH + 1 + tap, q), :] * cw_ref[tap:tap + 1, :]
    xact = acc * _sigmoid(acc)
    xs = xact[:, :D_INNER]
    bm = xact[:, D_INNER:D_INNER + SSM_GROUPS * SSM_STATE]
    cm = xact[:, D_INNER + SSM_GROUPS * SSM_STATE:]

    v = dtr_ref[...] + dtb_ref[...]
    dt = jnp.maximum(v, 0.0) + jnp.log1p(jnp.exp(-jnp.abs(v)))
    if valid_len is not None:
        t_idx = c * q + lax.broadcasted_iota(jnp.int32, dt.shape, 0)
        dt = jnp.where(t_idx < valid_len, dt, 0.0)
    a = -jnp.exp(alog_ref[...])
    r_i = lax.broadcasted_iota(jnp.int32, (q, q), 0)
    c_i = lax.broadcasted_iota(jnp.int32, (q, q), 1)
    tril = c_i <= r_i
    acs = _mm_01(tril.astype(BF16), dt * a, 3, left=True)
    acs_last = acs[q - 1:q, :]
    acs_t = acs.T
    e_r = lax.broadcasted_iota(jnp.int32, (LANES, D_INNER), 0)
    e_c = lax.broadcasted_iota(jnp.int32, (LANES, D_INNER), 1)
    expand = (e_c // hp == e_r).astype(BF16)
    dt_e = _mm_01(expand, dt, 2)
    to_end_e = _mm_01(expand, jnp.exp(acs_last - acs) * dt, 2)
    eacs_e = _mm_01(expand, jnp.exp(acs), 3)
    cdec_e = _mm_01(expand, jnp.exp(acs_last), 3)

    xdt_b = (xs * dt_e).astype(BF16)
    xw_b = (xs * to_end_e).astype(BF16)
    for g in range(SSM_GROUPS):
        bg = bm[:, g * SSM_STATE:(g + 1) * SSM_STATE]
        cg_b = cm[:, g * SSM_STATE:(g + 1) * SSM_STATE].astype(BF16)
        cb = _nt(cg_b, bg.astype(BF16))
        for hh in range(heads_per_group):
            h = g * heads_per_group + hh
            seg = acs[:, h:h + 1] - acs_t[h:h + 1, :]
            decay = jnp.where(tril, jnp.exp(jnp.minimum(seg, 0.0)), 0.0)
            yd_ref[:, h * hp:(h + 1) * hp] = _mm((cb * decay).astype(BF16),
                                                 xdt_b[:, h * hp:(h + 1) * hp])
        gc = slice(g * gw, (g + 1) * gw)
        st_g = st_ref[:, gc]
        y_off = _mm(cg_b, st_g.astype(BF16)) * eacs_e[:, gc]
        yd_ref[:, gc] += y_off
        st_ref[:, gc] = st_g * cdec_e[:, gc] + _mm(bg.T.astype(BF16), xw_b[:, gc])

    y = yd_ref[...] + dskip_ref[...] * xs
    zz = z_ref[...]
    gated = y * (zz * _sigmoid(zz))
    for g in range(SSM_GROUPS):
        gc = slice(g * gw, (g + 1) * gw)
        gg = gated[:, gc]
        ms = jnp.mean(gg * gg, axis=1, keepdims=True)
        y_ref[:, gc] = gg * lax.rsqrt(ms + NORM_EPS) * nw_ref[:, gc]

    @pl.when(c == nc - 1)
    def _():
        hfin_ref[0] = st_ref[...].T


def _ssd(xbc, z, dtr, conv0, h0, conv_w, conv_b, dt_bias, a_log, d_skip, norm_w, bsz, seq,
         valid_len=None):
    q = min(SSD_CHUNK, seq)
    nc = seq // q
    pad16 = lambda t: jnp.pad(t.reshape(1, SSM_HEADS), ((0, 0), (0, DT_PAD - SSM_HEADS)))
    row = lambda w: pl.BlockSpec((q, w), lambda b, c: (b * nc + c, 0))
    const = lambda shape: pl.BlockSpec(shape, lambda b, c: (0,) * len(shape))
    per_b = lambda shape: pl.BlockSpec((1,) + shape, lambda b, c: (b, 0, 0))
    return pl.pallas_call(
        functools.partial(_ssd_kernel, valid_len=valid_len),
        grid=(bsz, nc),
        in_specs=[row(CONV_CH), row(D_INNER), row(DT_PAD),
                  per_b((SUBLANES, CONV_CH)), per_b((D_INNER, SSM_STATE)),
                  const((CONV_WIDTH, CONV_CH)), const((1, CONV_CH)),
                  const((1, DT_PAD)), const((1, DT_PAD)), const((1, D_INNER)),
                  const((1, D_INNER))],
        out_specs=[row(D_INNER), per_b((D_INNER, SSM_STATE))],
        out_shape=[jax.ShapeDtypeStruct((bsz * seq, D_INNER), F32),
                   jax.ShapeDtypeStruct((bsz, D_INNER, SSM_STATE), F32)],
        scratch_shapes=[pltpu.VMEM((SUBLANES, CONV_CH), F32),
                        pltpu.VMEM((q + SUBLANES, CONV_CH), F32),
                        pltpu.VMEM((SSM_STATE, D_INNER), F32),
                        pltpu.VMEM((q, D_INNER), F32)],
        compiler_params=_params("parallel", "arbitrary"),
        name="ssd",
    )(xbc, z, dtr, conv0, h0, conv_w, conv_b.reshape(1, CONV_CH), pad16(dt_bias), pad16(a_log),
      jnp.repeat(d_skip, SSM_HEAD_DIM).reshape(1, D_INNER), norm_w.reshape(1, D_INNER))


def _layer_norm(x, w, b):
    mu = jnp.mean(x, axis=-1, keepdims=True)
    xc = x - mu
    var = jnp.mean(xc * xc, axis=-1, keepdims=True)
    return xc * lax.rsqrt(var + NORM_EPS) * w + b


def _out_proj_kernel(attn_ref, ssd_ref, x_ref, w_ref, lnw_ref, lnb_ref, wr2_ref, wr1_ref, br_ref,
                     *rest):
    h_ref, hb_ref, logit_ref = rest[-3:]
    mixed = (_mm(attn_ref[...].astype(BF16), w_ref[0:D_ATTN, :])
             + _mm(ssd_ref[...].astype(BF16), w_ref[D_ATTN:, :]))
    h = _layer_norm(DEEPNORM_ALPHA * x_ref[...] + mixed, lnw_ref[...], lnb_ref[...])
    h_ref[...] = h
    h_hi = h.astype(BF16)
    hb_ref[...] = h_hi
    h_lo = (h - h_hi.astype(F32)).astype(BF16)
    part = _mm(h_hi, wr2_ref[...])
    logit_ref[...] = part[:, :LANES] + part[:, LANES:] + _mm(h_lo, wr1_ref[...]) + br_ref[...]


def _out_proj(attn, ssd, x, w_bf16, lw, tm, n_rows, row_offset=0, into=None):
    t = x.shape[0]
    assert row_offset % tm == 0 and t % tm == 0
    off = row_offset // tm
    n_in = t // tm
    steps = n_in if into is not None else pl.cdiv(n_rows, tm)
    row = lambda w: pl.BlockSpec((tm, w), lambda i: (jnp.minimum(i, n_in - 1), 0))
    out_row = lambda w: pl.BlockSpec((tm, w), lambda i: (off + i, 0))
    const = lambda shape: pl.BlockSpec(shape, lambda i: (0, 0))
    in_specs = [row(D_ATTN), row(D_INNER), row(D_MODEL), const((D_MODEL, D_MODEL)),
                const((1, D_MODEL)), const((1, D_MODEL)), const((D_MODEL, 2 * LANES)),
                const((D_MODEL, LANES)), const((1, LANES))]
    args = [attn, ssd, x, w_bf16, lw["ln1_w"].reshape(1, D_MODEL), lw["ln1_b"].reshape(1, D_MODEL),
            lw["w_router_hl"], lw["w_router_hi"], lw["b_router_pad"]]
    aliases = {}
    if into is not None:
        aliases = {len(args) + j: j for j in range(len(into))}
        in_specs += [pl.BlockSpec(memory_space=pl.ANY)] * len(into)
        args += list(into)
    return pl.pallas_call(
        _out_proj_kernel,
        grid=(steps,),
        in_specs=in_specs,
        out_specs=[out_row(D_MODEL), out_row(D_MODEL), out_row(LANES)],
        out_shape=[jax.ShapeDtypeStruct((n_rows, D_MODEL), F32),
                   jax.ShapeDtypeStruct((n_rows, D_MODEL), BF16),
                   jax.ShapeDtypeStruct((n_rows, LANES), F32)],
        input_output_aliases=aliases,
        compiler_params=_params("parallel"),
        name="out_proj",
    )(*args)


MOE_BM = 512
MOE_TF = 1024


MOE_SUB = 128


def _row_cases(rows, compute, out_ref):
    for n in range(MOE_SUB, MOE_BM + 1, MOE_SUB):
        @pl.when((rows > n - MOE_SUB) & (rows <= n))
        def _():
            compute(slice(0, n))
            if n < MOE_BM:
                out_ref[n:, :] = jnp.zeros((MOE_BM - n, out_ref.shape[1]), out_ref.dtype)

    @pl.when(rows == 0)
    def _():
        out_ref[...] = jnp.zeros_like(out_ref)


def _gate_up_kernel(e_ref, n_ref, r_ref, first_ref, rows_ref, nxt_e_ref, nxt_n_ref, has_nxt_ref,
                    x_ref, w_hbm, bg_ref, bu_ref, act_ref, wg_stage, wu_stage, wgb_ref, wub_ref,
                    sem):
    i = pl.program_id(0)

    def tile_copies(e, n):
        gate = pltpu.make_async_copy(
            w_hbm.at[e, :, pl.ds(pl.multiple_of(n * MOE_TF, MOE_TF), MOE_TF)], wg_stage, sem.at[0])
        up = pltpu.make_async_copy(
            w_hbm.at[e, :, pl.ds(pl.multiple_of(D_FF + n * MOE_TF, MOE_TF), MOE_TF)], wu_stage,
            sem.at[1])
        return gate, up

    @pl.when(i == 0)
    def _():
        for c in tile_copies(e_ref[0], n_ref[0]):
            c.start()

    @pl.when(first_ref[i] == 1)
    def _():
        for c in tile_copies(e_ref[i], n_ref[i]):
            c.wait()
        wgb_ref[...] = wg_stage[...].astype(BF16)
        wub_ref[...] = wu_stage[...].astype(BF16)

        @pl.when(has_nxt_ref[i] == 1)
        def _():
            for c in tile_copies(nxt_e_ref[i], nxt_n_ref[i]):
                c.start()

    def compute(sl):
        x = x_ref[sl, :]
        g = jnp.minimum(_mm(x, wgb_ref[...]) + bg_ref[0], SWIGLU_LIMIT)
        u = jnp.clip(_mm(x, wub_ref[...]) + bu_ref[0], -SWIGLU_LIMIT, SWIGLU_LIMIT)
        act_ref[sl, :] = ((u + 1.0) * (g * _sigmoid(SWIGLU_ALPHA * g))).astype(BF16)

    _row_cases(rows_ref[i], compute, act_ref)


def _down_kernel(e_ref, first_ref, rows_ref, nxt_e_ref, has_nxt_ref, a_ref, w_hbm, bd_ref,
                 y_ref, wd_stage, wdb_ref, sem):
    i = pl.program_id(0)
    expert_copy = lambda e: pltpu.make_async_copy(w_hbm.at[e], wd_stage, sem.at[0])

    @pl.when(i == 0)
    def _():
        expert_copy(e_ref[0]).start()

    @pl.when(first_ref[i] == 1)
    def _():
        expert_copy(e_ref[i]).wait()
        wdb_ref[...] = wd_stage[...].astype(BF16)

        @pl.when(has_nxt_ref[i] == 1)
        def _():
            expert_copy(nxt_e_ref[i]).start()

    def compute(sl):
        y_ref[sl, :] = (_mm(a_ref[sl, :], wdb_ref[...]) + bd_ref[0]).astype(y_ref.dtype)

    _row_cases(rows_ref[i], compute, y_ref)


def _moe_experts(xs, sched, w_gate_up, b_gate_up, w_down, b_down):
    cap = xs.shape[0]
    n_blocks = cap // MOE_BM
    nt = D_FF // MOE_TF
    up_off = D_FF // MOE_TF
    b_gu = b_gate_up.reshape(N_EXPERTS, 1, 2 * D_FF)
    gu_spec = pltpu.PrefetchScalarGridSpec(
        num_scalar_prefetch=8,
        grid=(n_blocks * nt,),
        in_specs=[
            pl.BlockSpec((MOE_BM, D_MODEL), lambda i, e, n, r, *_: (r[i], 0)),
            pl.BlockSpec(memory_space=pl.ANY),
            pl.BlockSpec((1, 1, MOE_TF), lambda i, e, n, r, *_: (e[i], 0, n[i])),
            pl.BlockSpec((1, 1, MOE_TF), lambda i, e, n, r, *_: (e[i], 0, up_off + n[i])),
        ],
        out_specs=pl.BlockSpec((MOE_BM, MOE_TF), lambda i, e, n, r, *_: (r[i], n[i])),
        scratch_shapes=[pltpu.VMEM((D_MODEL, MOE_TF), F32), pltpu.VMEM((D_MODEL, MOE_TF), F32),
                        pltpu.VMEM((D_MODEL, MOE_TF), BF16), pltpu.VMEM((D_MODEL, MOE_TF), BF16),
                        pltpu.SemaphoreType.DMA((2,))])
    act = pl.pallas_call(
        _gate_up_kernel,
        grid_spec=gu_spec,
        out_shape=jax.ShapeDtypeStruct((cap, D_FF), BF16),
        compiler_params=_params("arbitrary"),
        name="moe_gate_up",
    )(sched["gu_e"], sched["gu_n"], sched["gu_r"], sched["gu_first"], sched["gu_rows"],
      sched["gu_nxt_e"], sched["gu_nxt_n"], sched["gu_has_nxt"], xs, w_gate_up, b_gu, b_gu)

    down_spec = pltpu.PrefetchScalarGridSpec(
        num_scalar_prefetch=5,
        grid=(n_blocks,),
        in_specs=[
            pl.BlockSpec((MOE_BM, D_FF), lambda i, e, *_: (i, 0)),
            pl.BlockSpec(memory_space=pl.ANY),
            pl.BlockSpec((1, 1, D_MODEL), lambda i, e, *_: (e[i], 0, 0)),
        ],
        out_specs=pl.BlockSpec((MOE_BM, D_MODEL), lambda i, e, *_: (i, 0)),
        scratch_shapes=[pltpu.VMEM((D_FF, D_MODEL), F32), pltpu.VMEM((D_FF, D_MODEL), BF16),
                        pltpu.SemaphoreType.DMA((1,))])
    return pl.pallas_call(
        _down_kernel,
        grid_spec=down_spec,
        out_shape=jax.ShapeDtypeStruct((cap, D_MODEL), BF16),
        compiler_params=_params("arbitrary"),
        name="moe_down",
    )(sched["d_e"], sched["d_first"], sched["d_rows"], sched["d_nxt_e"], sched["d_has_nxt"],
      act, w_down, b_down.reshape(N_EXPERTS, 1, D_MODEL))


def _route_kernel(logit_ref, idx_ref, gate_ref, pos_ref, cnt_ref, carry_ref):
    i = pl.program_id(0)
    tm = logit_ref.shape[0]

    @pl.when(i == 0)
    def _():
        carry_ref[...] = jnp.zeros_like(carry_ref)

    lane = lax.broadcasted_iota(jnp.int32, (tm, LANES), 1).astype(F32)
    lg = jnp.where(lane < N_EXPERTS, logit_ref[...], -jnp.inf)
    sel = jnp.zeros((tm, LANES), F32)
    vals, idxs = [], []
    for _ in range(TOP_K):
        mx = lg.max(axis=1, keepdims=True)
        ik = jnp.min(jnp.where(lg == mx, lane, float(LANES)), axis=1, keepdims=True)
        hit = lane == ik
        sel = sel + hit.astype(F32)
        lg = jnp.where(hit, -jnp.inf, lg)
        vals.append(mx)
        idxs.append(ik)
    ex = [jnp.exp(v - vals[0]) for v in vals]
    den = ex[0]
    for e in ex[1:]:
        den = den + e
    r_i = lax.broadcasted_iota(jnp.int32, (tm, tm), 0)
    c_i = lax.broadcasted_iota(jnp.int32, (tm, tm), 1)
    before = _mm((c_i < r_i).astype(BF16), sel.astype(BF16)) + carry_ref[...]
    idx_out = jnp.zeros((tm, LANES), F32)
    gate_out = jnp.zeros((tm, LANES), F32)
    pos_out = jnp.zeros((tm, LANES), F32)
    for kk in range(TOP_K):
        pos = jnp.sum(jnp.where(lane == idxs[kk], before, 0.0), axis=1, keepdims=True)
        idx_out = jnp.where(lane == kk, idxs[kk], idx_out)
        gate_out = jnp.where(lane == kk, ex[kk] / den, gate_out)
        pos_out = jnp.where(lane == kk, pos, pos_out)
    idx_ref[...] = idx_out.astype(jnp.int32)
    gate_ref[...] = gate_out
    pos_ref[...] = pos_out.astype(jnp.int32)
    carry_ref[...] += jnp.sum(sel, axis=0, keepdims=True)
    cnt_ref[...] = carry_ref[...].astype(jnp.int32)


def _route(logits, tm):
    t = logits.shape[0]
    row = pl.BlockSpec((tm, LANES), lambda i: (i, 0))
    return pl.pallas_call(
        _route_kernel,
        grid=(t // tm,),
        in_specs=[row],
        out_specs=[row, row, row, pl.BlockSpec((1, LANES), lambda i: (0, 0))],
        out_shape=[jax.ShapeDtypeStruct((t, LANES), jnp.int32),
                   jax.ShapeDtypeStruct((t, LANES), F32),
                   jax.ShapeDtypeStruct((t, LANES), jnp.int32),
                   jax.ShapeDtypeStruct((1, LANES), jnp.int32)],
        scratch_shapes=[pltpu.VMEM((1, LANES), F32)],
        compiler_params=_params("arbitrary"),
        name="moe_route",
    )(logits)


MOE_KEY_BITS = 17


def _moe_schedule(idx, pos, counts):
    i32 = jnp.int32
    n_tok = idx.shape[0]
    n_assign = n_tok * TOP_K
    low_mask = (1 << MOE_KEY_BITS) - 1
    assert n_assign < low_mask
    blocks_e = (counts + MOE_BM - 1) // MOE_BM
    blk_end = jnp.cumsum(blocks_e).astype(i32)
    blk_start = blk_end - blocks_e
    slot_of_assign = blk_start[idx] * MOE_BM + pos
    n_blocks = -(-n_assign // MOE_BM) + N_EXPERTS
    cap = n_blocks * MOE_BM
    count_ge = lambda x, ends: jnp.sum((x[:, None] >= ends[None, :]).astype(i32), axis=1)
    key_real = idx.reshape(-1) * (1 << MOE_KEY_BITS) + jnp.arange(n_assign, dtype=i32)
    pad_end = jnp.cumsum(blocks_e * MOE_BM - counts).astype(i32)
    pad_expert = count_ge(jnp.arange(cap - n_assign, dtype=i32), pad_end)
    key_pad = pad_expert * (1 << MOE_KEY_BITS) + low_mask
    low = jnp.sort(jnp.concatenate([key_real, key_pad])) & low_mask
    slot_tok = jnp.where(low == low_mask, 0, low // TOP_K)
    blk = jnp.arange(n_blocks, dtype=i32)
    blk_e = jnp.minimum(count_ge(blk, blk_end), N_EXPERTS - 1)
    blk_rows = jnp.clip(counts[blk_e] - (blk - blk_start[blk_e]) * MOE_BM, 0, MOE_BM)
    nblk_e = blocks_e.at[N_EXPERTS - 1].add(n_blocks - blk_end[-1])
    d_nxt = blk_start[blk_e] + nblk_e[blk_e]
    nt = D_FF // MOE_TF
    n_items = n_blocks * nt
    item = jnp.arange(n_items, dtype=i32)
    it_e = jnp.minimum(count_ge(item, jnp.cumsum(nblk_e * nt).astype(i32)), N_EXPERTS - 1)
    local_item = item - blk_start[it_e] * nt
    per = jnp.maximum(nblk_e[it_e], 1)
    it_n = local_item // per
    it_lr = local_item % per
    it_r = blk_start[it_e] + it_lr
    it_nxt = item - it_lr + per
    sched = {"gu_e": it_e, "gu_n": it_n, "gu_r": it_r, "gu_first": (it_lr == 0).astype(i32),
             "gu_rows": blk_rows[it_r],
             "gu_nxt_e": it_e[jnp.minimum(it_nxt, n_items - 1)],
             "gu_nxt_n": it_n[jnp.minimum(it_nxt, n_items - 1)],
             "gu_has_nxt": (it_nxt < n_items).astype(i32),
             "d_e": blk_e, "d_first": (blk == blk_start[blk_e]).astype(i32), "d_rows": blk_rows,
             "d_nxt_e": blk_e[jnp.minimum(d_nxt, n_blocks - 1)],
             "d_has_nxt": (d_nxt < n_blocks).astype(i32)}
    return slot_tok, slot_of_assign, sched


def _combine_kernel(y_ref, srt_ref, slot_ref, g_ref, h_ref, lnw_ref, lnb_ref, op_ref, os_ref, *,
                    n_first):
    i = pl.program_id(0)
    g = g_ref[...]
    slot = slot_ref[...]
    srt = srt_ref[0]
    w = jnp.where(slot[:, 0:1] == srt, g[:, 0:1], 0.0)
    for kk in range(1, TOP_K):
        w = w + jnp.where(slot[:, kk:kk + 1] == srt, g[:, kk:kk + 1], 0.0)
    w_hi = w.astype(BF16)
    w_lo = (w - w_hi.astype(F32)).astype(BF16)
    y = y_ref[...]
    ff = _mm(w_hi, y) + _mm(w_lo, y)
    out = _layer_norm(DEEPNORM_ALPHA * h_ref[...] + ff, lnw_ref[...], lnb_ref[...])

    @pl.when(i < n_first)
    def _():
        op_ref[...] = out

    @pl.when(i >= n_first)
    def _():
        os_ref[...] = out


def _combine(y_sorted, slot_tiles, slots, gates, h, ln_w, ln_b, n_p, tm):
    t = h.shape[0]
    assert n_p % tm == 0 and t % tm == 0
    n_first = n_p // tm
    row = lambda w: pl.BlockSpec((tm, w), lambda i: (i, 0))
    const = lambda shape: pl.BlockSpec(shape, lambda i: (0, 0))
    return pl.pallas_call(
        functools.partial(_combine_kernel, n_first=n_first),
        grid=(t // tm,),
        in_specs=[pl.BlockSpec((tm * TOP_K, D_MODEL), lambda i: (i, 0)),
                  pl.BlockSpec((1, 1, tm * TOP_K), lambda i: (i, 0, 0)), row(LANES), row(LANES),
                  row(D_MODEL), const((1, D_MODEL)), const((1, D_MODEL))],
        out_specs=[pl.BlockSpec((tm, D_MODEL), lambda i: (jnp.minimum(i, n_first - 1), 0)),
                   pl.BlockSpec((tm, D_MODEL), lambda i: (jnp.maximum(i - n_first, 0), 0))],
        out_shape=[jax.ShapeDtypeStruct((n_p, D_MODEL), F32),
                   jax.ShapeDtypeStruct((t - n_p, D_MODEL), F32)],
        compiler_params=_params("arbitrary"),
        name="moe_combine",
    )(y_sorted, slot_tiles, slots, gates, h, ln_w.reshape(1, D_MODEL), ln_b.reshape(1, D_MODEL))


def _mixer(x, w_in_b, w_out_b, lw, attend, conv0, h0, bsz, seq, tm, n_rows, row_offset=0,
           into=None):
    q, k, v, z, xbc, dtr = _in_proj(x, w_in_b, tm)
    attn = attend(q, k, v)
    if seq % SSD_CHUNK == 0:
        ssd, h_fin = _ssd(xbc, z, dtr, conv0, h0, lw["conv_w"], lw["conv_b"], lw["dt_bias"],
                          lw["a_log"], lw["d_skip"], lw["ssm_norm_w"], bsz, seq)
    else:
        assert seq < SSD_CHUNK
        pad = lambda t: jnp.pad(t.reshape(bsz, seq, -1),
                                ((0, 0), (0, SSD_CHUNK - seq), (0, 0))).reshape(bsz * SSD_CHUNK, -1)
        ssd, h_fin = _ssd(pad(xbc), pad(z), pad(dtr), conv0, h0, lw["conv_w"], lw["conv_b"],
                          lw["dt_bias"], lw["a_log"], lw["d_skip"], lw["ssm_norm_w"], bsz,
                          SSD_CHUNK, valid_len=seq)
        ssd = ssd.reshape(bsz, SSD_CHUNK, D_INNER)[:, :seq].reshape(bsz * seq, D_INNER)
    bufs = _out_proj(attn, ssd, x, w_out_b, lw, tm, n_rows, row_offset, into)
    return bufs, k, v, xbc, h_fin


def kernel(x_prompt, x_sample, cache_k, cache_v, state_conv, state_ssm, page_table, w_in, conv_w,
           conv_b, dt_bias, a_log, d_skip, ssm_norm_w, w_out, ln1_w, ln1_b, w_router, b_router,
           w_gate_up, b_gate_up, w_down, b_down, ln2_w, ln2_b):
    depth = w_in.shape[0]
    assert depth == 1
    bsz, seq, _ = x_prompt.shape
    dbs, dseq, _ = x_sample.shape
    n_pool = cache_k.shape[1]
    tail = CONV_WIDTH - 1
    n_p, n_s = bsz * seq, dbs * dseq
    n_all = n_p + n_s
    wr = jnp.pad(w_router[0], ((0, 0), (0, LANES - N_EXPERTS)))
    wr_hi = wr.astype(BF16)
    wr_lo = (wr - wr_hi.astype(F32)).astype(BF16)
    lw = {"conv_w": conv_w[0], "conv_b": conv_b[0], "dt_bias": dt_bias[0], "a_log": a_log[0],
          "d_skip": d_skip[0], "ssm_norm_w": ssm_norm_w[0], "ln1_w": ln1_w[0], "ln1_b": ln1_b[0],
          "w_router_hl": jnp.concatenate([wr_hi, wr_lo], axis=1), "w_router_hi": wr_hi,
          "b_router_pad": jnp.pad(b_router[0], (0, LANES - N_EXPERTS)).reshape(1, LANES)}
    w_in_b = jnp.pad(w_in[0], ((0, 0), (0, DT_PAD - SSM_HEADS))).astype(BF16)
    w_out_b = w_out[0].astype(BF16)

    conv0_p = jnp.zeros((bsz, SUBLANES, CONV_CH), F32)
    h0_p = jnp.zeros((bsz, D_INNER, SSM_STATE), F32)
    bufs, kp, vp, xbc_p, hfin_p = _mixer(
        x_prompt.reshape(n_p, D_MODEL), w_in_b, w_out_b, lw,
        lambda q, k, v: _moba_prompt(q, k, v, bsz, seq), conv0_p, h0_p, bsz, seq, tm=256,
        n_rows=n_all)

    conv0_s = jnp.pad(state_conv[0], ((0, 0), (SUBLANES - tail, 0), (0, 0)))
    h0_s = state_ssm[0].reshape(dbs, D_INNER, SSM_STATE)
    ck = cache_k[0].reshape(n_pool, PAGE_SIZE * N_HEADS, HEAD_DIM)
    cv = cache_v[0].reshape(n_pool, PAGE_SIZE * N_HEADS, HEAD_DIM)

    def attend_sample(q, k, v):
        new = lambda t: t.reshape(dbs, dseq, D_ATTN)
        return _moba_sample(new(q), new(k), new(v), ck, cv, page_table).reshape(n_s, D_ATTN)

    (h1, h1b, logits), ks, vs, xbc_s, hfin_s = _mixer(
        x_sample.reshape(n_s, D_MODEL), w_in_b, w_out_b, lw, attend_sample, conv0_s, h0_s,
        dbs, dseq, tm=n_s, n_rows=n_all, row_offset=n_p, into=bufs)

    tm_c = 128
    tm_r = 3 * tm_c if n_all % (3 * tm_c) == 0 else tm_c
    idx, gates, pos, counts = _route(logits, tm_r)
    slot_tok, slot_of_assign, sched = _moe_schedule(idx[:, :TOP_K], pos[:, :TOP_K],
                                                    counts[0, :N_EXPERTS])
    yb = _moe_experts(h1b[slot_tok], sched, w_gate_up[0], b_gate_up[0], w_down[0], b_down[0])
    slot_tiles = jnp.sort(slot_of_assign.reshape(n_all // tm_c, tm_c * TOP_K), axis=-1)
    y_sorted = yb[slot_tiles.reshape(-1)]
    slots = jnp.pad(slot_of_assign.astype(F32), ((0, 0), (0, LANES - TOP_K)),
                    constant_values=-1.0)
    out_p, out_s = _combine(y_sorted, slot_tiles.astype(F32)[:, None, :], slots, gates, h1,
                            ln2_w[0], ln2_b[0], n_p, tm_c)

    y_prompt = out_p.reshape(bsz, seq, D_MODEL)
    y_sample = out_s.reshape(dbs, dseq, D_MODEL)
    heads = lambda t, b, s: t.reshape(1, b, s, N_HEADS, HEAD_DIM)
    k_prompt, v_prompt = heads(kp, bsz, seq), heads(vp, bsz, seq)
    conv_prompt = xbc_p.reshape(bsz, seq, CONV_CH)[:, seq - tail:][None]
    ssm_prompt = hfin_p.reshape(1, bsz, SSM_HEADS, SSM_HEAD_DIM, SSM_STATE)
    k_sample, v_sample = heads(ks, dbs, dseq), heads(vs, dbs, dseq)
    conv_sample = jnp.concatenate([state_conv[0], xbc_s.reshape(dbs, dseq, CONV_CH)],
                                  axis=1)[:, dseq:][None]
    ssm_sample = hfin_s.reshape(1, dbs, SSM_HEADS, SSM_HEAD_DIM, SSM_STATE)
    return (y_prompt, y_sample, k_prompt, v_prompt, conv_prompt, ssm_prompt,
            k_sample, v_sample, conv_sample, ssm_sample)
```

```python
import functools
import math

import jax
import jax.numpy as jnp
from jax import lax
from jax.experimental import pallas as pl
from jax.experimental.pallas import tpu as pltpu

F32 = jnp.float32
BF16 = jnp.bfloat16
HIGHEST = lax.Precision.HIGHEST

D_MODEL = 2048
N_HEADS = 8
HEAD_DIM = 128
D_ATTN = N_HEADS * HEAD_DIM
MOBA_BLOCK = 256
MOBA_TOPK = 3
D_INNER = 1024
SSM_HEAD_DIM = 64
SSM_HEADS = 16
SSM_GROUPS = 2
SSM_STATE = 128
CONV_WIDTH = 4
CONV_CH = D_INNER + 2 * SSM_GROUPS * SSM_STATE
SSD_CHUNK = 128
N_EXPERTS = 32
TOP_K = 4
D_FF = 2048
SWIGLU_LIMIT = 7.0
SWIGLU_ALPHA = 1.702
NORM_EPS = 1e-5
DEEPNORM_ALPHA = 2.0 ** 0.25
PAGE_SIZE = 128

LANES = 128
SUBLANES = 8
VMEM_LIMIT_BYTES = 56 * 1024 * 1024

DT_PAD = LANES
NEG_BIG = -1e30


def _params(*sem):
    return pltpu.CompilerParams(dimension_semantics=sem, vmem_limit_bytes=VMEM_LIMIT_BYTES)


def _nt(a, b, precision=None):
    return lax.dot_general(a, b, (((1,), (1,)), ((), ())), precision=precision,
                           preferred_element_type=F32)


def _mm(a, b, precision=None):
    return jnp.dot(a, b, precision=precision, preferred_element_type=F32)


def _mm_01(m01, x, pieces, left=False):
    out = None
    rest = x
    for _ in range(pieces):
        part = rest.astype(BF16)
        rest = rest - part.astype(F32)
        term = _mm(m01, part) if left else _mm(part, m01)
        out = term if out is None else out + term
    return out


def _sigmoid(x):
    return 1.0 / (1.0 + jnp.exp(-x))


def _in_proj_kernel(x_ref, w_ref, q_ref, k_ref, v_ref, z_ref, xbc_ref, dt_ref):
    xb = x_ref[...].astype(BF16)
    col = 0
    for ref in (q_ref, k_ref, v_ref, z_ref, xbc_ref, dt_ref):
        width = ref.shape[1]
        ref[...] = _mm(xb, w_ref[:, col:col + width])
        col += width


def _in_proj(x, w_bf16, tm):
    t = x.shape[0]
    widths = (D_ATTN, D_ATTN, D_ATTN, D_INNER, CONV_CH, DT_PAD)
    return pl.pallas_call(
        _in_proj_kernel,
        grid=(t // tm,),
        in_specs=[pl.BlockSpec((tm, D_MODEL), lambda i: (i, 0)),
                  pl.BlockSpec(w_bf16.shape, lambda i: (0, 0), pipeline_mode=pl.Buffered(1))],
        out_specs=[pl.BlockSpec((tm, w), lambda i: (i, 0)) for w in widths],
        out_shape=[jax.ShapeDtypeStruct((t, w), F32) for w in widths],
        compiler_params=_params("parallel"),
        name="in_proj",
    )(x, w_bf16)


def _moba_prompt_kernel(q_ref, k_ref, v_ref, o_ref):
    seq = q_ref.shape[0]
    nb = seq // MOBA_BLOCK
    assert nb <= SUBLANES
    scale = 1.0 / math.sqrt(HEAD_DIM)
    q = q_ref[...]
    k = k_ref[...]
    kmean = jnp.sum(k.reshape(nb, MOBA_BLOCK, HEAD_DIM), axis=1) * (1.0 / MOBA_BLOCK)
    kmean = jnp.concatenate([kmean, jnp.zeros((LANES - nb, HEAD_DIM), F32)], axis=0)
    gate = _nt(kmean, q, HIGHEST)[0:SUBLANES]
    own = lax.broadcasted_iota(jnp.int32, (SUBLANES, seq), 1) // MOBA_BLOCK
    blk = lax.broadcasted_iota(jnp.int32, (SUBLANES, seq), 0)
    rank = jnp.zeros((SUBLANES, seq), jnp.int32)
    for jj in range(nb):
        gj = gate[jj:jj + 1, :]
        ahead = ((gj > gate) | ((gj == gate) & (blk > jj))) & (own > jj)
        rank = rank + ahead.astype(jnp.int32)
    keep = ((blk < own) & (rank < MOBA_TOPK)) | (blk == own)
    pen = jnp.where(keep, 0.0, NEG_BIG)
    pen_t = jnp.concatenate([pen, jnp.zeros((LANES - SUBLANES, seq), F32)], axis=0).T
    lane = lax.broadcasted_iota(jnp.int32, (seq, LANES), 1)
    key_blk = lax.broadcasted_iota(jnp.int32, (seq, LANES), 0) // MOBA_BLOCK
    q_aug = jnp.concatenate([q.astype(BF16), pen_t.astype(BF16)], axis=1)
    k_aug = jnp.concatenate([k.astype(BF16), (lane == key_blk).astype(BF16)], axis=1)
    v_aug = jnp.concatenate([v_ref[...].astype(BF16), (lane == 0).astype(BF16)], axis=1)
    row = lax.broadcasted_iota(jnp.int32, (MOBA_BLOCK, MOBA_BLOCK), 0)
    col = lax.broadcasted_iota(jnp.int32, (MOBA_BLOCK, MOBA_BLOCK), 1)
    causal = col <= row
    for i in range(nb):
        rows = slice(i * MOBA_BLOCK, (i + 1) * MOBA_BLOCK)
        s_blocks = []
        for j in range(i + 1):
            s = _nt(q_aug[rows], k_aug[j * MOBA_BLOCK:(j + 1) * MOBA_BLOCK]) * scale
            if j == i:
                s = jnp.where(causal, s, NEG_BIG)
            s_blocks.append(s)
        mx = s_blocks[0]
        for s in s_blocks[1:]:
            mx = jnp.maximum(mx, s)
        m = mx.max(axis=1, keepdims=True)
        o = jnp.zeros((MOBA_BLOCK, 2 * HEAD_DIM), F32)
        for j, s in enumerate(s_blocks):
            p = jnp.exp(s - m).astype(BF16)
            o = o + _mm(p, v_aug[j * MOBA_BLOCK:(j + 1) * MOBA_BLOCK])
        o_ref[rows, :] = o[:, :HEAD_DIM] / o[:, HEAD_DIM:HEAD_DIM + 1]


def _moba_prompt(q, k, v, bsz, seq):
    spec = pl.BlockSpec((seq, HEAD_DIM), lambda b, h: (b, h))
    return pl.pallas_call(
        _moba_prompt_kernel,
        grid=(bsz, N_HEADS),
        in_specs=[spec, spec, spec],
        out_specs=spec,
        out_shape=jax.ShapeDtypeStruct((bsz * seq, D_ATTN), F32),
        compiler_params=_params("parallel", "parallel"),
        name="moba_prompt",
    )(q, k, v)


SAMPLE_PAGES_PER_STEP = 8


def _moba_sample_kernel(pt_ref, q_ref, kn_ref, vn_ref, *rest, n_new, pps):
    del pt_ref
    k_refs, v_refs = rest[:pps], rest[pps:2 * pps]
    o_ref, st_ref, ksum_ref, acc_ref, l_ref = rest[2 * pps:]
    ph = pl.program_id(1)
    s = pl.program_id(2)
    n_steps = pl.num_programs(2)
    n_pages = st_ref.shape[0]
    ppb = MOBA_BLOCK // PAGE_SIZE
    n_blocks = n_pages // ppb
    scale = 1.0 / math.sqrt(HEAD_DIM)
    head_rows = lambda ref, h: ref[0, pl.ds(h, PAGE_SIZE, stride=N_HEADS), :]

    @pl.when((ph == 0) & (s == 0))
    def _():
        ksum_ref[...] = jnp.zeros_like(ksum_ref)

    @pl.when(ph == 0)
    def _():
        for bi in range(pps // ppb):
            ksum = jnp.zeros((N_HEADS, HEAD_DIM), F32)
            for i in range(bi * ppb, (bi + 1) * ppb):
                pg = s * pps + i
                ksum = ksum + jnp.sum(k_refs[i][0].reshape(PAGE_SIZE, N_HEADS, HEAD_DIM), axis=0)
                for h in range(N_HEADS):
                    kh = head_rows(k_refs[i], h).astype(BF16)
                    st_ref[pg, h] = _nt(q_ref[0, h].astype(BF16), kh) * scale
            blk = s * (pps // ppb) + bi
            for h in range(N_HEADS):
                ksum_ref[h, pl.ds(blk, 1), :] = ksum[h:h + 1, :]

    @pl.when((ph == 1) & (s == 0))
    def _():
        q_idx = lax.broadcasted_iota(jnp.int32, (SUBLANES, 1), 0)
        blk_id = lax.broadcasted_iota(jnp.int32, (SUBLANES, LANES), 1)
        pens = []
        for h in range(N_HEADS):
            gate = _nt(q_ref[0, h], ksum_ref[h] * (1.0 / MOBA_BLOCK), HIGHEST)
            rank = jnp.zeros(gate.shape, jnp.int32)
            for jj in range(n_blocks):
                gj = gate[:, jj:jj + 1]
                ahead = (gj > gate) | ((gj == gate) & (blk_id > jj))
                rank = rank + ahead.astype(jnp.int32)
            sel = (rank < MOBA_TOPK) & (blk_id < n_blocks)
            pens.append(jnp.where(sel, 0.0, NEG_BIG))
        n_keys = n_pages * PAGE_SIZE
        lane_blk = lax.broadcasted_iota(jnp.int32, (LANES, n_keys), 1) // MOBA_BLOCK
        row_blk = lax.broadcasted_iota(jnp.int32, (LANES, n_keys), 0)
        pen = _mm(jnp.concatenate(pens, axis=0).astype(BF16), (lane_blk == row_blk).astype(BF16))
        for h in range(N_HEADS):
            qh = q_ref[0, h]
            pen_h = lambda pg: pen[h * SUBLANES:(h + 1) * SUBLANES,
                                   pg * PAGE_SIZE:(pg + 1) * PAGE_SIZE]
            mx = jnp.full((SUBLANES, PAGE_SIZE), NEG_BIG, F32)
            for pg in range(n_pages):
                mx = jnp.maximum(mx, st_ref[pg, h] + pen_h(pg))
            m = mx.max(axis=1, keepdims=True)
            s_new = []
            for kk in range(n_new):
                sn = jnp.sum(qh * kn_ref[0, h, kk:kk + 1, :], axis=1, keepdims=True) * scale
                sn = jnp.where(q_idx >= kk, sn, NEG_BIG)
                s_new.append(sn)
                m = jnp.maximum(m, sn)
            l = jnp.zeros((SUBLANES, 1), F32)
            acc = jnp.zeros((SUBLANES, HEAD_DIM), F32)
            for kk in range(n_new):
                pn = jnp.exp(s_new[kk] - m)
                l = l + pn
                acc = acc + pn * vn_ref[0, h, kk:kk + 1, :]
            lsum = jnp.zeros((SUBLANES, PAGE_SIZE), F32)
            for pg in range(n_pages):
                pj = jnp.exp(st_ref[pg, h] + pen_h(pg) - m)
                lsum = lsum + pj
                st_ref[pg, h] = pj
            l_ref[h] = l + lsum.sum(axis=1, keepdims=True)
            acc_ref[h] = acc

    @pl.when(ph == 1)
    def _():
        for h in range(N_HEADS):
            acc = acc_ref[h]
            for i in range(pps):
                vh = head_rows(v_refs[i], h).astype(BF16)
                acc = acc + _mm(st_ref[s * pps + i, h].astype(BF16), vh)
            acc_ref[h] = acc

    @pl.when((ph == 1) & (s == n_steps - 1))
    def _():
        for h in range(N_HEADS):
            o_ref[0, h] = acc_ref[h] / l_ref[h]


def _moba_sample(q, k_new, v_new, cache_k, cache_v, page_table):
    bsz, n_new, _ = q.shape
    n_pages = page_table.shape[1]
    pps = SAMPLE_PAGES_PER_STEP
    assert n_new <= SUBLANES and n_pages % pps == 0 and pps % (MOBA_BLOCK // PAGE_SIZE) == 0
    assert n_pages * PAGE_SIZE // MOBA_BLOCK <= LANES
    n_steps = n_pages // pps

    def head_tiles(t):
        t = t.reshape(bsz, n_new, N_HEADS, HEAD_DIM).transpose(0, 2, 1, 3)
        return jnp.pad(t, ((0, 0), (0, 0), (0, SUBLANES - n_new), (0, 0)))

    def k_map(i):
        return lambda b, ph, s, pt: (
            pt[b * n_pages + jnp.where(ph == 0, s * pps + i, n_pages - pps + i)], 0, 0)

    def v_map(i):
        return lambda b, ph, s, pt: (pt[b * n_pages + jnp.where(ph == 1, s * pps + i, i)], 0, 0)

    tile = pl.BlockSpec((1, N_HEADS, SUBLANES, HEAD_DIM), lambda b, ph, s, pt: (b, 0, 0, 0))
    page = lambda index_map: pl.BlockSpec((1, PAGE_SIZE * N_HEADS, HEAD_DIM), index_map)
    grid_spec = pltpu.PrefetchScalarGridSpec(
        num_scalar_prefetch=1,
        grid=(bsz, 2, n_steps),
        in_specs=[tile, tile, tile] + [page(k_map(i)) for i in range(pps)]
        + [page(v_map(i)) for i in range(pps)],
        out_specs=tile,
        scratch_shapes=[pltpu.VMEM((n_pages, N_HEADS, SUBLANES, PAGE_SIZE), F32),
                        pltpu.VMEM((N_HEADS, LANES, HEAD_DIM), F32),
                        pltpu.VMEM((N_HEADS, SUBLANES, HEAD_DIM), F32),
                        pltpu.VMEM((N_HEADS, SUBLANES, 1), F32)])
    o = pl.pallas_call(
        functools.partial(_moba_sample_kernel, n_new=n_new, pps=pps),
        grid_spec=grid_spec,
        out_shape=jax.ShapeDtypeStruct((bsz, N_HEADS, SUBLANES, HEAD_DIM), F32),
        compiler_params=_params("parallel", "arbitrary", "arbitrary"),
        name="moba_sample",
    )(page_table.reshape(-1), head_tiles(q), head_tiles(k_new), head_tiles(v_new),
      *([cache_k] * pps), *([cache_v] * pps))
    return o[:, :, :n_new].transpose(0, 2, 1, 3).reshape(bsz, n_new, D_ATTN)


def _ssd_kernel(xbc_ref, z_ref, dtr_ref, conv0_ref, h0_ref, cw_ref, cb_ref, dtb_ref, alog_ref,
                dskip_ref, nw_ref, y_ref, hfin_ref, tail_ref, xx_ref, st_ref, yd_ref, *, valid_len):
    c = pl.program_id(1)
    nc = pl.num_programs(1)
    q = xbc_ref.shape[0]
    hp = SSM_HEAD_DIM
    gw = D_INNER // SSM_GROUPS
    heads_per_group = SSM_HEADS // SSM_GROUPS

    @pl.when(c == 0)
    def _():
        tail_ref[...] = conv0_ref[0]
        st_ref[...] = h0_ref[0].T

    xbc = xbc_ref[...]
    xx_ref[0:SUBLANES, :] = tail_ref[...]
    xx_ref[SUBLANES:SUBLANES + q, :] = xbc
    tail_ref[...] = xbc[q - SUBLANES:q, :]
    acc = cb_ref[...] + xx_ref[pl.ds(SUBLANES - CONV_WIDTH + 1, q), :] * cw_ref[0:1, :]
    for tap in range(1, CONV_WIDTH):
        acc = acc + xx_ref[pl.ds(SUBLANES - CONV_WIDTH + 1 + tap, q), :] * cw_ref[tap:tap + 1, :]
    xact = acc * _sigmoid(acc)
    xs = xact[:, :D_INNER]
    bm = xact[:, D_INNER:D_INNER + SSM_GROUPS * SSM_STATE]
    cm = xact[:, D_INNER + SSM_GROUPS * SSM_STATE:]

    v = dtr_ref[...] + dtb_ref[...]
    dt = jnp.maximum(v, 0.0) + jnp.log1p(jnp.exp(-jnp.abs(v)))
    if valid_len is not None:
        t_idx = c * q + lax.broadcasted_iota(jnp.int32, dt.shape, 0)
        dt = jnp.where(t_idx < valid_len, dt, 0.0)
    a = -jnp.exp(alog_ref[...])
    r_i = lax.broadcasted_iota(jnp.int32, (q, q), 0)
    c_i = lax.broadcasted_iota(jnp.int32, (q, q), 1)
    tril = c_i <= r_i
    acs = _mm_01(tril.astype(BF16), dt * a, 3, left=True)
    acs_last = acs[q - 1:q, :]
    acs_t = acs.T
    e_r = lax.broadcasted_iota(jnp.int32, (LANES, D_INNER), 0)
    e_c = lax.broadcasted_iota(jnp.int32, (LANES, D_INNER), 1)
    expand = (e_c // hp == e_r).astype(BF16)
    dt_e = _mm_01(expand, dt, 2)
    to_end_e = _mm_01(expand, jnp.exp(acs_last - acs) * dt, 2)
    eacs_e = _mm_01(expand, jnp.exp(acs), 3)
    cdec_e = _mm_01(expand, jnp.exp(acs_last), 3)

    xdt_b = (xs * dt_e).astype(BF16)
    xw_b = (xs * to_end_e).astype(BF16)
    for g in range(SSM_GROUPS):
        bg = bm[:, g * SSM_STATE:(g + 1) * SSM_STATE]
        cg_b = cm[:, g * SSM_STATE:(g + 1) * SSM_STATE].astype(BF16)
        cb = _nt(cg_b, bg.astype(BF16))
        for hh in range(heads_per_group):
            h = g * heads_per_group + hh
            seg = acs[:, h:h + 1] - acs_t[h:h + 1, :]
            decay = jnp.where(tril, jnp.exp(jnp.minimum(seg, 0.0)), 0.0)
            yd_ref[:, h * hp:(h + 1) * hp] = _mm((cb * decay).astype(BF16),
                                                 xdt_b[:, h * hp:(h + 1) * hp])
        gc = slice(g * gw, (g + 1) * gw)
        st_g = st_ref[:, gc]
        y_off = _mm(cg_b, st_g.astype(BF16)) * eacs_e[:, gc]
        yd_ref[:, gc] += y_off
        st_ref[:, gc] = st_g * cdec_e[:, gc] + _mm(bg.T.astype(BF16), xw_b[:, gc])

    y = yd_ref[...] + dskip_ref[...] * xs
    zz = z_ref[...]
    gated = y * (zz * _sigmoid(zz))
    for g in range(SSM_GROUPS):
        gc = slice(g * gw, (g + 1) * gw)
        gg = gated[:, gc]
        ms = jnp.mean(gg * gg, axis=1, keepdims=True)
        y_ref[:, gc] = gg * lax.rsqrt(ms + NORM_EPS) * nw_ref[:, gc]

    @pl.when(c == nc - 1)
    def _():
        hfin_ref[0] = st_ref[...].T


def _ssd(xbc, z, dtr, conv0, h0, conv_w, conv_b, dt_bias, a_log, d_skip, norm_w, bsz, seq,
         valid_len=None):
    q = min(SSD_CHUNK, seq)
    nc = seq // q
    pad16 = lambda t: jnp.pad(t.reshape(1, SSM_HEADS), ((0, 0), (0, DT_PAD - SSM_HEADS)))
    row = lambda w: pl.BlockSpec((q, w), lambda b, c: (b * nc + c, 0))
    const = lambda shape: pl.BlockSpec(shape, lambda b, c: (0,) * len(shape))
    per_b = lambda shape: pl.BlockSpec((1,) + shape, lambda b, c: (b, 0, 0))
    return pl.pallas_call(
        functools.partial(_ssd_kernel, valid_len=valid_len),
        grid=(bsz, nc),
        in_specs=[row(CONV_CH), row(D_INNER), row(DT_PAD),
                  per_b((SUBLANES, CONV_CH)), per_b((D_INNER, SSM_STATE)),
                  const((CONV_WIDTH, CONV_CH)), const((1, CONV_CH)),
                  const((1, DT_PAD)), const((1, DT_PAD)), const((1, D_INNER)),
                  const((1, D_INNER))],
        out_specs=[row(D_INNER), per_b((D_INNER, SSM_STATE))],
        out_shape=[jax.ShapeDtypeStruct((bsz * seq, D_INNER), F32),
                   jax.ShapeDtypeStruct((bsz, D_INNER, SSM_STATE), F32)],
        scratch_shapes=[pltpu.VMEM((SUBLANES, CONV_CH), F32),
                        pltpu.VMEM((q + SUBLANES, CONV_CH), F32),
                        pltpu.VMEM((SSM_STATE, D_INNER), F32),
                        pltpu.VMEM((q, D_INNER), F32)],
        compiler_params=_params("parallel", "arbitrary"),
        name="ssd",
    )(xbc, z, dtr, conv0, h0, conv_w, conv_b.reshape(1, CONV_CH), pad16(dt_bias), pad16(a_log),
      jnp.repeat(d_skip, SSM_HEAD_DIM).reshape(1, D_INNER), norm_w.reshape(1, D_INNER))


def _layer_norm(x, w, b):
    mu = jnp.mean(x, axis=-1, keepdims=True)
    xc = x - mu
    var = jnp.mean(xc * xc, axis=-1, keepdims=True)
    return xc * lax.rsqrt(var + NORM_EPS) * w + b


def _out_proj_kernel(attn_ref, ssd_ref, x_ref, w_ref, lnw_ref, lnb_ref, wr2_ref, wr1_ref, br_ref,
                     *rest):
    h_ref, hb_ref, logit_ref = rest[-3:]
    mixed = (_mm(attn_ref[...].astype(BF16), w_ref[0:D_ATTN, :])
             + _mm(ssd_ref[...].astype(BF16), w_ref[D_ATTN:, :]))
    h = _layer_norm(DEEPNORM_ALPHA * x_ref[...] + mixed, lnw_ref[...], lnb_ref[...])
    h_ref[...] = h
    h_hi = h.astype(BF16)
    hb_ref[...] = h_hi
    h_lo = (h - h_hi.astype(F32)).astype(BF16)
    part = _mm(h_hi, wr2_ref[...])
    logit_ref[...] = part[:, :LANES] + part[:, LANES:] + _mm(h_lo, wr1_ref[...]) + br_ref[...]


def _out_proj(attn, ssd, x, w_bf16, lw, tm, n_rows, row_offset=0, into=None):
    t = x.shape[0]
    assert row_offset % tm == 0 and t % tm == 0
    off = row_offset // tm
    n_in = t // tm
    steps = n_in if into is not None else pl.cdiv(n_rows, tm)
    row = lambda w: pl.BlockSpec((tm, w), lambda i: (jnp.minimum(i, n_in - 1), 0))
    out_row = lambda w: pl.BlockSpec((tm, w), lambda i: (off + i, 0))
    const = lambda shape: pl.BlockSpec(shape, lambda i: (0, 0))
    in_specs = [row(D_ATTN), row(D_INNER), row(D_MODEL), const((D_MODEL, D_MODEL)),
                const((1, D_MODEL)), const((1, D_MODEL)), const((D_MODEL, 2 * LANES)),
                const((D_MODEL, LANES)), const((1, LANES))]
    args = [attn, ssd, x, w_bf16, lw["ln1_w"].reshape(1, D_MODEL), lw["ln1_b"].reshape(1, D_MODEL),
            lw["w_router_hl"], lw["w_router_hi"], lw["b_router_pad"]]
    aliases = {}
    if into is not None:
        aliases = {len(args) + j: j for j in range(len(into))}
        in_specs += [pl.BlockSpec(memory_space=pl.ANY)] * len(into)
        args += list(into)
    return pl.pallas_call(
        _out_proj_kernel,
        grid=(steps,),
        in_specs=in_specs,
        out_specs=[out_row(D_MODEL), out_row(D_MODEL), out_row(LANES)],
        out_shape=[jax.ShapeDtypeStruct((n_rows, D_MODEL), F32),
                   jax.ShapeDtypeStruct((n_rows, D_MODEL), BF16),
                   jax.ShapeDtypeStruct((n_rows, LANES), F32)],
        input_output_aliases=aliases,
        compiler_params=_params("parallel"),
        name="out_proj",
    )(*args)


MOE_BM = 256
MOE_TF = 1024


MOE_SUB = 128


def _row_cases(rows, compute, out_ref):
    for n in range(MOE_SUB, MOE_BM + 1, MOE_SUB):
        @pl.when((rows > n - MOE_SUB) & (rows <= n))
        def _():
            compute(slice(0, n))
            if n < MOE_BM:
                out_ref[n:, :] = jnp.zeros((MOE_BM - n, out_ref.shape[1]), out_ref.dtype)

    @pl.when(rows == 0)
    def _():
        out_ref[...] = jnp.zeros_like(out_ref)


def _gate_up_kernel(e_ref, n_ref, r_ref, first_ref, rows_ref, nxt_e_ref, nxt_n_ref, has_nxt_ref,
                    x_ref, w_hbm, bg_ref, bu_ref, act_ref, wg_stage, wu_stage, wgb_ref, wub_ref,
                    sem):
    i = pl.program_id(0)

    def tile_copies(e, n):
        gate = pltpu.make_async_copy(
            w_hbm.at[e, :, pl.ds(pl.multiple_of(n * MOE_TF, MOE_TF), MOE_TF)], wg_stage, sem.at[0])
        up = pltpu.make_async_copy(
            w_hbm.at[e, :, pl.ds(pl.multiple_of(D_FF + n * MOE_TF, MOE_TF), MOE_TF)], wu_stage,
            sem.at[1])
        return gate, up

    @pl.when(i == 0)
    def _():
        for c in tile_copies(e_ref[0], n_ref[0]):
            c.start()

    @pl.when(first_ref[i] == 1)
    def _():
        for c in tile_copies(e_ref[i], n_ref[i]):
            c.wait()
        wgb_ref[...] = wg_stage[...].astype(BF16)
        wub_ref[...] = wu_stage[...].astype(BF16)

        @pl.when(has_nxt_ref[i] == 1)
        def _():
            for c in tile_copies(nxt_e_ref[i], nxt_n_ref[i]):
                c.start()

    def compute(sl):
        x = x_ref[sl, :]
        g = jnp.minimum(_mm(x, wgb_ref[...]) + bg_ref[0], SWIGLU_LIMIT)
        u = jnp.clip(_mm(x, wub_ref[...]) + bu_ref[0], -SWIGLU_LIMIT, SWIGLU_LIMIT)
        act_ref[sl, :] = ((u + 1.0) * (g * _sigmoid(SWIGLU_ALPHA * g))).astype(BF16)

    _row_cases(rows_ref[i], compute, act_ref)


def _down_kernel(e_ref, first_ref, rows_ref, nxt_e_ref, has_nxt_ref, a_ref, w_hbm, bd_ref,
                 y_ref, wd_stage, wdb_ref, sem):
    i = pl.program_id(0)
    expert_copy = lambda e: pltpu.make_async_copy(w_hbm.at[e], wd_stage, sem.at[0])

    @pl.when(i == 0)
    def _():
        expert_copy(e_ref[0]).start()

    @pl.when(first_ref[i] == 1)
    def _():
        expert_copy(e_ref[i]).wait()
        wdb_ref[...] = wd_stage[...].astype(BF16)

        @pl.when(has_nxt_ref[i] == 1)
        def _():
            expert_copy(nxt_e_ref[i]).start()

    def compute(sl):
        y_ref[sl, :] = (_mm(a_ref[sl, :], wdb_ref[...]) + bd_ref[0]).astype(y_ref.dtype)

    _row_cases(rows_ref[i], compute, y_ref)


def _moe_experts(xs, sched, w_gate_up, b_gate_up, w_down, b_down):
    cap = xs.shape[0]
    n_blocks = cap // MOE_BM
    nt = D_FF // MOE_TF
    up_off = D_FF // MOE_TF
    b_gu = b_gate_up.reshape(N_EXPERTS, 1, 2 * D_FF)
    gu_spec = pltpu.PrefetchScalarGridSpec(
        num_scalar_prefetch=8,
        grid=(n_blocks * nt,),
        in_specs=[
            pl.BlockSpec((MOE_BM, D_MODEL), lambda i, e, n, r, *_: (r[i], 0)),
            pl.BlockSpec(memory_space=pl.ANY),
            pl.BlockSpec((1, 1, MOE_TF), lambda i, e, n, r, *_: (e[i], 0, n[i])),
            pl.BlockSpec((1, 1, MOE_TF), lambda i, e, n, r, *_: (e[i], 0, up_off + n[i])),
        ],
        out_specs=pl.BlockSpec((MOE_BM, MOE_TF), lambda i, e, n, r, *_: (r[i], n[i])),
        scratch_shapes=[pltpu.VMEM((D_MODEL, MOE_TF), F32), pltpu.VMEM((D_MODEL, MOE_TF), F32),
                        pltpu.VMEM((D_MODEL, MOE_TF), BF16), pltpu.VMEM((D_MODEL, MOE_TF), BF16),
                        pltpu.SemaphoreType.DMA((2,))])
    act = pl.pallas_call(
        _gate_up_kernel,
        grid_spec=gu_spec,
        out_shape=jax.ShapeDtypeStruct((cap, D_FF), BF16),
        compiler_params=_params("arbitrary"),
        name="moe_gate_up",
    )(sched["gu_e"], sched["gu_n"], sched["gu_r"], sched["gu_first"], sched["gu_rows"],
      sched["gu_nxt_e"], sched["gu_nxt_n"], sched["gu_has_nxt"], xs, w_gate_up, b_gu, b_gu)

    down_spec = pltpu.PrefetchScalarGridSpec(
        num_scalar_prefetch=5,
        grid=(n_blocks,),
        in_specs=[
            pl.BlockSpec((MOE_BM, D_FF), lambda i, e, *_: (i, 0)),
            pl.BlockSpec(memory_space=pl.ANY),
            pl.BlockSpec((1, 1, D_MODEL), lambda i, e, *_: (e[i], 0, 0)),
        ],
        out_specs=pl.BlockSpec((MOE_BM, D_MODEL), lambda i, e, *_: (i, 0)),
        scratch_shapes=[pltpu.VMEM((D_FF, D_MODEL), F32), pltpu.VMEM((D_FF, D_MODEL), BF16),
                        pltpu.SemaphoreType.DMA((1,))])
    return pl.pallas_call(
        _down_kernel,
        grid_spec=down_spec,
        out_shape=jax.ShapeDtypeStruct((cap, D_MODEL), BF16),
        compiler_params=_params("arbitrary"),
        name="moe_down",
    )(sched["d_e"], sched["d_first"], sched["d_rows"], sched["d_nxt_e"], sched["d_has_nxt"],
      act, w_down, b_down.reshape(N_EXPERTS, 1, D_MODEL))


def _route_kernel(logit_ref, idx_ref, gate_ref, pos_ref, cnt_ref, carry_ref):
    i = pl.program_id(0)
    tm = logit_ref.shape[0]

    @pl.when(i == 0)
    def _():
        carry_ref[...] = jnp.zeros_like(carry_ref)

    lane = lax.broadcasted_iota(jnp.int32, (tm, LANES), 1).astype(F32)
    lg = jnp.where(lane < N_EXPERTS, logit_ref[...], -jnp.inf)
    sel = jnp.zeros((tm, LANES), F32)
    vals, idxs = [], []
    for _ in range(TOP_K):
        mx = lg.max(axis=1, keepdims=True)
        ik = jnp.min(jnp.where(lg == mx, lane, float(LANES)), axis=1, keepdims=True)
        hit = lane == ik
        sel = sel + hit.astype(F32)
        lg = jnp.where(hit, -jnp.inf, lg)
        vals.append(mx)
        idxs.append(ik)
    ex = [jnp.exp(v - vals[0]) for v in vals]
    den = ex[0]
    for e in ex[1:]:
        den = den + e
    r_i = lax.broadcasted_iota(jnp.int32, (tm, tm), 0)
    c_i = lax.broadcasted_iota(jnp.int32, (tm, tm), 1)
    before = _mm((c_i < r_i).astype(BF16), sel.astype(BF16)) + carry_ref[...]
    idx_out = jnp.zeros((tm, LANES), F32)
    gate_out = jnp.zeros((tm, LANES), F32)
    pos_out = jnp.zeros((tm, LANES), F32)
    for kk in range(TOP_K):
        pos = jnp.sum(jnp.where(lane == idxs[kk], before, 0.0), axis=1, keepdims=True)
        idx_out = jnp.where(lane == kk, idxs[kk], idx_out)
        gate_out = jnp.where(lane == kk, ex[kk] / den, gate_out)
        pos_out = jnp.where(lane == kk, pos, pos_out)
    idx_ref[...] = idx_out.astype(jnp.int32)
    gate_ref[...] = gate_out
    pos_ref[...] = pos_out.astype(jnp.int32)
    carry_ref[...] += jnp.sum(sel, axis=0, keepdims=True)
    cnt_ref[...] = carry_ref[...].astype(jnp.int32)


def _route(logits, tm):
    t = logits.shape[0]
    row = pl.BlockSpec((tm, LANES), lambda i: (i, 0))
    return pl.pallas_call(
        _route_kernel,
        grid=(t // tm,),
        in_specs=[row],
        out_specs=[row, row, row, pl.BlockSpec((1, LANES), lambda i: (0, 0))],
        out_shape=[jax.ShapeDtypeStruct((t, LANES), jnp.int32),
                   jax.ShapeDtypeStruct((t, LANES), F32),
                   jax.ShapeDtypeStruct((t, LANES), jnp.int32),
                   jax.ShapeDtypeStruct((1, LANES), jnp.int32)],
        scratch_shapes=[pltpu.VMEM((1, LANES), F32)],
        compiler_params=_params("arbitrary"),
        name="moe_route",
    )(logits)


MOE_KEY_BITS = 17


def _moe_schedule(idx, pos, counts):
    i32 = jnp.int32
    n_tok = idx.shape[0]
    n_assign = n_tok * TOP_K
    low_mask = (1 << MOE_KEY_BITS) - 1
    assert n_assign < low_mask
    blocks_e = (counts + MOE_BM - 1) // MOE_BM
    blk_end = jnp.cumsum(blocks_e).astype(i32)
    blk_start = blk_end - blocks_e
    slot_of_assign = blk_start[idx] * MOE_BM + pos
    n_blocks = -(-n_assign // MOE_BM) + N_EXPERTS
    cap = n_blocks * MOE_BM
    count_ge = lambda x, ends: jnp.sum((x[:, None] >= ends[None, :]).astype(i32), axis=1)
    key_real = idx.reshape(-1) * (1 << MOE_KEY_BITS) + jnp.arange(n_assign, dtype=i32)
    pad_end = jnp.cumsum(blocks_e * MOE_BM - counts).astype(i32)
    pad_expert = count_ge(jnp.arange(cap - n_assign, dtype=i32), pad_end)
    key_pad = pad_expert * (1 << MOE_KEY_BITS) + low_mask
    low = jnp.sort(jnp.concatenate([key_real, key_pad])) & low_mask
    slot_tok = jnp.where(low == low_mask, 0, low // TOP_K)
    blk = jnp.arange(n_blocks, dtype=i32)
    blk_e = jnp.minimum(count_ge(blk, blk_end), N_EXPERTS - 1)
    blk_rows = jnp.clip(counts[blk_e] - (blk - blk_start[blk_e]) * MOE_BM, 0, MOE_BM)
    nblk_e = blocks_e.at[N_EXPERTS - 1].add(n_blocks - blk_end[-1])
    d_nxt = blk_start[blk_e] + nblk_e[blk_e]
    nt = D_FF // MOE_TF
    n_items = n_blocks * nt
    item = jnp.arange(n_items, dtype=i32)
    it_e = jnp.minimum(count_ge(item, jnp.cumsum(nblk_e * nt).astype(i32)), N_EXPERTS - 1)
    local_item = item - blk_start[it_e] * nt
    per = jnp.maximum(nblk_e[it_e], 1)
    it_n = local_item // per
    it_lr = local_item % per
    it_r = blk_start[it_e] + it_lr
    it_nxt = item - it_lr + per
    sched = {"gu_e": it_e, "gu_n": it_n, "gu_r": it_r, "gu_first": (it_lr == 0).astype(i32),
             "gu_rows": blk_rows[it_r],
             "gu_nxt_e": it_e[jnp.minimum(it_nxt, n_items - 1)],
             "gu_nxt_n": it_n[jnp.minimum(it_nxt, n_items - 1)],
             "gu_has_nxt": (it_nxt < n_items).astype(i32),
             "d_e": blk_e, "d_first": (blk == blk_start[blk_e]).astype(i32), "d_rows": blk_rows,
             "d_nxt_e": blk_e[jnp.minimum(d_nxt, n_blocks - 1)],
             "d_has_nxt": (d_nxt < n_blocks).astype(i32)}
    return slot_tok, slot_of_assign, sched


def _combine_kernel(y_ref, srt_ref, slot_ref, g_ref, h_ref, lnw_ref, lnb_ref, op_ref, os_ref, *,
                    n_first):
    i = pl.program_id(0)
    g = g_ref[...]
    slot = slot_ref[...]
    srt = srt_ref[0]
    w = jnp.where(slot[:, 0:1] == srt, g[:, 0:1], 0.0)
    for kk in range(1, TOP_K):
        w = w + jnp.where(slot[:, kk:kk + 1] == srt, g[:, kk:kk + 1], 0.0)
    w_hi = w.astype(BF16)
    w_lo = (w - w_hi.astype(F32)).astype(BF16)
    y = y_ref[...]
    ff = _mm(w_hi, y) + _mm(w_lo, y)
    out = _layer_norm(DEEPNORM_ALPHA * h_ref[...] + ff, lnw_ref[...], lnb_ref[...])

    @pl.when(i < n_first)
    def _():
        op_ref[...] = out

    @pl.when(i >= n_first)
    def _():
        os_ref[...] = out


def _combine(y_sorted, slot_tiles, slots, gates, h, ln_w, ln_b, n_p, tm):
    t = h.shape[0]
    assert n_p % tm == 0 and t % tm == 0
    n_first = n_p // tm
    row = lambda w: pl.BlockSpec((tm, w), lambda i: (i, 0))
    const = lambda shape: pl.BlockSpec(shape, lambda i: (0, 0))
    return pl.pallas_call(
        functools.partial(_combine_kernel, n_first=n_first),
        grid=(t // tm,),
        in_specs=[pl.BlockSpec((tm * TOP_K, D_MODEL), lambda i: (i, 0)),
                  pl.BlockSpec((1, 1, tm * TOP_K), lambda i: (i, 0, 0)), row(LANES), row(LANES),
                  row(D_MODEL), const((1, D_MODEL)), const((1, D_MODEL))],
        out_specs=[pl.BlockSpec((tm, D_MODEL), lambda i: (jnp.minimum(i, n_first - 1), 0)),
                   pl.BlockSpec((tm, D_MODEL), lambda i: (jnp.maximum(i - n_first, 0), 0))],
        out_shape=[jax.ShapeDtypeStruct((n_p, D_MODEL), F32),
                   jax.ShapeDtypeStruct((t - n_p, D_MODEL), F32)],
        compiler_params=_params("arbitrary"),
        name="moe_combine",
    )(y_sorted, slot_tiles, slots, gates, h, ln_w.reshape(1, D_MODEL), ln_b.reshape(1, D_MODEL))


def _mixer(x, w_in_b, w_out_b, lw, attend, conv0, h0, bsz, seq, tm, n_rows, row_offset=0,
           into=None):
    q, k, v, z, xbc, dtr = _in_proj(x, w_in_b, tm)
    attn = attend(q, k, v)
    if seq % SSD_CHUNK == 0:
        ssd, h_fin = _ssd(xbc, z, dtr, conv0, h0, lw["conv_w"], lw["conv_b"], lw["dt_bias"],
                          lw["a_log"], lw["d_skip"], lw["ssm_norm_w"], bsz, seq)
    else:
        assert seq < SSD_CHUNK
        pad = lambda t: jnp.pad(t.reshape(bsz, seq, -1),
                                ((0, 0), (0, SSD_CHUNK - seq), (0, 0))).reshape(bsz * SSD_CHUNK, -1)
        ssd, h_fin = _ssd(pad(xbc), pad(z), pad(dtr), conv0, h0, lw["conv_w"], lw["conv_b"],
                          lw["dt_bias"], lw["a_log"], lw["d_skip"], lw["ssm_norm_w"], bsz,
                          SSD_CHUNK, valid_len=seq)
        ssd = ssd.reshape(bsz, SSD_CHUNK, D_INNER)[:, :seq].reshape(bsz * seq, D_INNER)
    bufs = _out_proj(attn, ssd, x, w_out_b, lw, tm, n_rows, row_offset, into)
    return bufs, k, v, xbc, h_fin


def kernel(x_prompt, x_sample, cache_k, cache_v, state_conv, state_ssm, page_table, w_in, conv_w,
           conv_b, dt_bias, a_log, d_skip, ssm_norm_w, w_out, ln1_w, ln1_b, w_router, b_router,
           w_gate_up, b_gate_up, w_down, b_down, ln2_w, ln2_b):
    depth = w_in.shape[0]
    assert depth == 1
    bsz, seq, _ = x_prompt.shape
    dbs, dseq, _ = x_sample.shape
    n_pool = cache_k.shape[1]
    tail = CONV_WIDTH - 1
    n_p, n_s = bsz * seq, dbs * dseq
    n_all = n_p + n_s
    wr = jnp.pad(w_router[0], ((0, 0), (0, LANES - N_EXPERTS)))
    wr_hi = wr.astype(BF16)
    wr_lo = (wr - wr_hi.astype(F32)).astype(BF16)
    lw = {"conv_w": conv_w[0], "conv_b": conv_b[0], "dt_bias": dt_bias[0], "a_log": a_log[0],
          "d_skip": d_skip[0], "ssm_norm_w": ssm_norm_w[0], "ln1_w": ln1_w[0], "ln1_b": ln1_b[0],
          "w_router_hl": jnp.concatenate([wr_hi, wr_lo], axis=1), "w_router_hi": wr_hi,
          "b_router_pad": jnp.pad(b_router[0], (0, LANES - N_EXPERTS)).reshape(1, LANES)}
    w_in_b = jnp.pad(w_in[0], ((0, 0), (0, DT_PAD - SSM_HEADS))).astype(BF16)
    w_out_b = w_out[0].astype(BF16)

    conv0_p = jnp.zeros((bsz, SUBLANES, CONV_CH), F32)
    h0_p = jnp.zeros((bsz, D_INNER, SSM_STATE), F32)
    bufs, kp, vp, xbc_p, hfin_p = _mixer(
        x_prompt.reshape(n_p, D_MODEL), w_in_b, w_out_b, lw,
        lambda q, k, v: _moba_prompt(q, k, v, bsz, seq), conv0_p, h0_p, bsz, seq, tm=256,
        n_rows=n_all)

    conv0_s = jnp.pad(state_conv[0], ((0, 0), (SUBLANES - tail, 0), (0, 0)))
    h0_s = state_ssm[0].reshape(dbs, D_INNER, SSM_STATE)
    ck = cache_k[0].reshape(n_pool, PAGE_SIZE * N_HEADS, HEAD_DIM)
    cv = cache_v[0].reshape(n_pool, PAGE_SIZE * N_HEADS, HEAD_DIM)

    def attend_sample(q, k, v):
        new = lambda t: t.reshape(dbs, dseq, D_ATTN)
        return _moba_sample(new(q), new(k), new(v), ck, cv, page_table).reshape(n_s, D_ATTN)

    (h1, h1b, logits), ks, vs, xbc_s, hfin_s = _mixer(
        x_sample.reshape(n_s, D_MODEL), w_in_b, w_out_b, lw, attend_sample, conv0_s, h0_s,
        dbs, dseq, tm=n_s, n_rows=n_all, row_offset=n_p, into=bufs)

    tm_c = 128
    tm_r = 3 * tm_c if n_all % (3 * tm_c) == 0 else tm_c
    idx, gates, pos, counts = _route(logits, tm_r)
    slot_tok, slot_of_assign, sched = _moe_schedule(idx[:, :TOP_K], pos[:, :TOP_K],
                                                    counts[0, :N_EXPERTS])
    yb = _moe_experts(h1b[slot_tok], sched, w_gate_up[0], b_gate_up[0], w_down[0], b_down[0])
    slot_tiles = jnp.sort(slot_of_assign.reshape(n_all // tm_c, tm_c * TOP_K), axis=-1)
    y_sorted = yb[slot_tiles.reshape(-1)]
    slots = jnp.pad(slot_of_assign.astype(F32), ((0, 0), (0, LANES - TOP_K)),
                    constant_values=-1.0)
    out_p, out_s = _combine(y_sorted, slot_tiles.astype(F32)[:, None, :], slots, gates, h1,
                            ln2_w[0], ln2_b[0], n_p, tm_c)

    y_prompt = out_p.reshape(bsz, seq, D_MODEL)
    y_sample = out_s.reshape(dbs, dseq, D_MODEL)
    heads = lambda t, b, s: t.reshape(1, b, s, N_HEADS, HEAD_DIM)
    k_prompt, v_prompt = heads(kp, bsz, seq), heads(vp, bsz, seq)
    conv_prompt = xbc_p.reshape(bsz, seq, CONV_CH)[:, seq - tail:][None]
    ssm_prompt = hfin_p.reshape(1, bsz, SSM_HEADS, SSM_HEAD_DIM, SSM_STATE)
    k_sample, v_sample = heads(ks, dbs, dseq), heads(vs, dbs, dseq)
    conv_sample = jnp.concatenate([state_conv[0], xbc_s.reshape(dbs, dseq, CONV_CH)],
                                  axis=1)[:, dseq:][None]
    ssm_sample = hfin_s.reshape(1, dbs, SSM_HEADS, SSM_HEAD_DIM, SSM_STATE)
    return (y_prompt, y_sample, k_prompt, v_prompt, conv_prompt, ssm_prompt,
            k_sample, v_sample, conv_sample, ssm_sample)
```
